```python
import math
import jax, jax.numpy as jnp
from jax import lax
import numpy as np

D_MODEL = 1024
BATCH = 4
SEQ = 8192
DEPTH = 2

N_MIXERS = 2
POOL_WINDOWS = (2, 4, 8, 16)
POOL_GROUPS = len(POOL_WINDOWS)
POOL_GROUP_DIM = D_MODEL // POOL_GROUPS
RET_HEADS = 4
RET_QK_DIM = D_MODEL // RET_HEADS
RET_V_DIM = 2 * D_MODEL // RET_HEADS
RET_QK_TOTAL = RET_HEADS * RET_QK_DIM
RET_V_TOTAL = RET_HEADS * RET_V_DIM
RET_IN_DIM = 2 * RET_QK_TOTAL + 2 * RET_V_TOTAL
RET_CHUNK = 128
ROPE_BASE = 10000.0
D_FF = 2816
CONV_WIDTH = 3
EPS = 1e-6
N_NORMS = 4
N_POOL_LAYERS = (DEPTH + N_MIXERS - 1) // N_MIXERS
N_RET_LAYERS = DEPTH // N_MIXERS

kernel_name = "hybrid_pool_retention_convglu"


def rms_norm(x, gain):
    xf = x.astype(jnp.float32)
    var = jnp.mean(xf * xf, axis=-1, keepdims=True)
    return (xf * lax.rsqrt(var + EPS) * gain.astype(jnp.float32)).astype(x.dtype)


def pool_mixer(h, w_groups, scale):
    b, s, d = h.shape
    hf = h.astype(jnp.float32)
    cs = jnp.cumsum(hf, axis=1)
    t = jnp.arange(s)
    parts = []
    for g, win in enumerate(POOL_WINDOWS):
        lo, hi = g * POOL_GROUP_DIM, (g + 1) * POOL_GROUP_DIM
        c = cs[..., lo:hi]
        c_shift = jnp.pad(c, ((0, 0), (win, 0), (0, 0)))[:, :s]
        count = jnp.minimum(t + 1, win).astype(jnp.float32)[None, :, None]
        parts.append((c - c_shift) / count - hf[..., lo:hi])
    diff = jnp.concatenate(parts, axis=-1).reshape(b, s, POOL_GROUPS, POOL_GROUP_DIM)
    y = jnp.einsum('bsgc,gcd->bsgd', diff, w_groups.astype(jnp.float32)).reshape(b, s, d)
    return (y * scale.astype(jnp.float32)).astype(h.dtype)


def rotary(x, positions):
    dh = x.shape[-1]
    inv_freq = ROPE_BASE ** (-jnp.arange(0, dh, 2, dtype=jnp.float32) / dh)
    ang = positions.astype(jnp.float32)[..., None] * inv_freq
    cos = jnp.cos(ang)[:, :, None, :]
    sin = jnp.sin(ang)[:, :, None, :]
    x1, x2 = x[..., : dh // 2], x[..., dh // 2:]
    return jnp.concatenate([x1 * cos - x2 * sin, x2 * cos + x1 * sin], axis=-1)


def retention(h, positions, w_in, gn_gain, w_out):
    b, s, _ = h.shape
    n_chunks = s // RET_CHUNK
    proj = (h @ w_in).astype(jnp.float32)
    q, k, v, g = jnp.split(proj, [RET_QK_TOTAL, 2 * RET_QK_TOTAL, 2 * RET_QK_TOTAL + RET_V_TOTAL], axis=-1)
    q = rotary(q.reshape(b, s, RET_HEADS, RET_QK_DIM), positions) * (RET_QK_DIM ** -0.5)
    k = rotary(k.reshape(b, s, RET_HEADS, RET_QK_DIM), positions)
    v = v.reshape(b, s, RET_HEADS, RET_V_DIM)

    def to_chunks(a):
        return a.reshape(b, n_chunks, RET_CHUNK, RET_HEADS, a.shape[-1]).transpose(1, 0, 3, 2, 4)

    qc, kc, vc = to_chunks(q), to_chunks(k), to_chunks(v)

    log_gamma = jnp.log(1.0 - 2.0 ** (-5.0 - jnp.arange(RET_HEADS, dtype=jnp.float32)))
    i = jnp.arange(RET_CHUNK, dtype=jnp.float32)
    rel = i[:, None] - i[None, :]
    intra_decay = jnp.where(rel >= 0, jnp.exp(jnp.maximum(rel, 0.0) * log_gamma[:, None, None]), 0.0)
    cross_decay = jnp.exp((i + 1.0) * log_gamma[:, None])[None, :, :, None]
    in_decay = jnp.exp((RET_CHUNK - 1.0 - i) * log_gamma[:, None])[None, :, :, None]
    chunk_decay = jnp.exp(RET_CHUNK * log_gamma)[None, :, None, None]

    def step(state, xs):
        qb, kb, vb = xs
        scores = jnp.einsum('bhid,bhjd->bhij', qb, kb) * intra_decay[None]
        out = jnp.einsum('bhij,bhjv->bhiv', scores, vb) \
            + jnp.einsum('bhid,bhdv->bhiv', qb, state) * cross_decay
        state = state * chunk_decay + jnp.einsum('bhjd,bhjv->bhdv', kb * in_decay, vb)
        return state, out

    state0 = jnp.zeros((b, RET_HEADS, RET_QK_DIM, RET_V_DIM), jnp.float32)
    _, o = lax.scan(step, state0, (qc, kc, vc))
    o = o.transpose(1, 0, 3, 2, 4).reshape(b, s, RET_HEADS, RET_V_DIM)
    mu = jnp.mean(o, axis=-1, keepdims=True)
    var = jnp.mean(jnp.square(o - mu), axis=-1, keepdims=True)
    o = ((o - mu) * lax.rsqrt(var + EPS)).reshape(b, s, RET_V_TOTAL) * gn_gain.astype(jnp.float32)
    y = jax.nn.silu(g) * o
    return (y.astype(h.dtype) @ w_out)


def conv_glu(h, w_up, conv_w, conv_b, w_down):
    s = h.shape[1]
    u = h @ w_up
    up = jnp.pad(u, ((0, 0), (CONV_WIDTH - 1, 0), (0, 0)))
    c = conv_b
    for tap in range(CONV_WIDTH):
        c = c + up[:, tap:tap + s] * conv_w[tap]
    gate, val = jnp.split(c, 2, axis=-1)
    return (jax.nn.silu(gate) * val) @ w_down


def setup_inputs(seed: int = 0) -> dict:
    key = jax.random.key(seed)
    ks = jax.random.split(key, 14)
    f32 = jnp.float32
    x = jax.random.normal(ks[0], (BATCH, SEQ, D_MODEL), f32)
    offsets = jax.random.randint(ks[1], (BATCH, 1), 0, 1024, dtype=jnp.int32)
    positions = (jnp.arange(SEQ, dtype=jnp.int32)[None, :] + offsets).astype(jnp.int32)
    norm_gain = 1.0 + 0.05 * jax.random.normal(ks[2], (DEPTH, N_NORMS, D_MODEL), f32)
    pool_w = jax.random.normal(ks[3], (N_POOL_LAYERS, POOL_GROUPS, POOL_GROUP_DIM, POOL_GROUP_DIM), f32) * POOL_GROUP_DIM ** -0.5
    pool_scale = 1.0 + 0.1 * jax.random.normal(ks[4], (N_POOL_LAYERS, D_MODEL), f32)
    ret_w_in = jax.random.normal(ks[5], (N_RET_LAYERS, D_MODEL, RET_IN_DIM), f32) * D_MODEL ** -0.5
    ret_gn_gain = 1.0 + 0.05 * jax.random.normal(ks[6], (N_RET_LAYERS, RET_V_TOTAL), f32)
    ret_w_out = jax.random.normal(ks[7], (N_RET_LAYERS, RET_V_TOTAL, D_MODEL), f32) * RET_V_TOTAL ** -0.5
    mlp_w_up = jax.random.normal(ks[8], (DEPTH, D_MODEL, 2 * D_FF), f32) * D_MODEL ** -0.5
    mlp_conv_w = jax.random.normal(ks[9], (DEPTH, CONV_WIDTH, 2 * D_FF), f32) * CONV_WIDTH ** -0.5
    mlp_conv_b = 0.02 * jax.random.normal(ks[10], (DEPTH, 2 * D_FF), f32)
    mlp_w_down = jax.random.normal(ks[11], (DEPTH, D_FF, D_MODEL), f32) * D_FF ** -0.5
    return {"x": x, "positions": positions, "norm_gain": norm_gain,
            "pool_w": pool_w, "pool_scale": pool_scale,
            "ret_w_in": ret_w_in, "ret_gn_gain": ret_gn_gain, "ret_w_out": ret_w_out,
            "mlp_w_up": mlp_w_up, "mlp_conv_w": mlp_conv_w, "mlp_conv_b": mlp_conv_b,
            "mlp_w_down": mlp_w_down}


def reference(x, positions, norm_gain, pool_w, pool_scale, ret_w_in, ret_gn_gain,
              ret_w_out, mlp_w_up, mlp_conv_w, mlp_conv_b, mlp_w_down):
    for layer in range(DEPTH):
        gains = norm_gain[layer]
        h = rms_norm(x, gains[0])
        j = layer // N_MIXERS
        if layer % N_MIXERS == 0:
            m = pool_mixer(h, pool_w[j], pool_scale[j])
        else:
            m = retention(h, positions, ret_w_in[j], ret_gn_gain[j], ret_w_out[j])
        x = x + rms_norm(m, gains[1])
        h = rms_norm(x, gains[2])
        f = conv_glu(h, mlp_w_up[layer], mlp_conv_w[layer], mlp_conv_b[layer], mlp_w_down[layer])
        x = x + rms_norm(f, gains[3])
    return x
```

```python
import functools

import jax
import jax.numpy as jnp
from jax import lax
from jax.experimental import pallas as pl
from jax.experimental.pallas import tpu as pltpu

D_MODEL = 1024
POOL_WINDOWS = (2, 4, 8, 16)
POOL_GROUP_DIM = D_MODEL // len(POOL_WINDOWS)
POOL_HALO = 16
RET_HEADS = 4
RET_QK_DIM = 256
RET_V_DIM = 512
RET_QK_TOTAL = RET_HEADS * RET_QK_DIM
RET_V_TOTAL = RET_HEADS * RET_V_DIM
RET_CHUNK = 256
ROPE_BASE = 10000.0
D_FF = 2816
FF_CHUNK = 256
CONV_HALO = 8
EPS = 1e-6

SEQ_TILE = 512
VMEM_LIMIT_BYTES = 56 * 1024 * 1024

_F32 = jnp.float32
_BF16 = jnp.bfloat16


def _dot(a, b):
    return jnp.dot(a, b, preferred_element_type=_F32)


def _rms(x, gain):
    var = jnp.mean(x * x, axis=-1, keepdims=True)
    return x * lax.rsqrt(var + EPS) * gain


def _silu(x):
    return x * (1.0 / (1.0 + jnp.exp(-x)))


def _pool_kernel(x_ref, gain_ref, w_ref, scale_ref, o_ref, ext_ref, *, tile):
    s = pl.program_id(1)
    x = x_ref[...]
    h = _rms(x, gain_ref[0:1, :])

    @pl.when(s == 0)
    def _():
        ext_ref[0:POOL_HALO, :] = jnp.zeros((POOL_HALO, D_MODEL), _F32)

    ext_ref[POOL_HALO:POOL_HALO + tile, :] = h
    t = s * tile + lax.broadcasted_iota(jnp.int32, (tile, 1), 0)
    parts = []
    for g, win in enumerate(POOL_WINDOWS):
        lo = g * POOL_GROUP_DIM
        hg = h[:, lo:lo + POOL_GROUP_DIM]
        acc = hg
        for j in range(1, win):
            acc = acc + ext_ref[POOL_HALO - j:POOL_HALO - j + tile, lo:lo + POOL_GROUP_DIM]
        inv_count = 1.0 / jnp.minimum(t + 1, win).astype(_F32)
        diff = acc * inv_count - hg
        parts.append(_dot(diff.astype(_BF16), w_ref[g]))
    y = jnp.concatenate(parts, axis=-1) * scale_ref[...]
    ext_ref[0:POOL_HALO, :] = h[tile - POOL_HALO:, :]
    o_ref[...] = x + _rms(y, gain_ref[1:2, :])


def _pool_layer(x, gains, w, scale):
    b, s, d = x.shape
    tile = SEQ_TILE
    const = lambda *_: (0, 0)
    return pl.pallas_call(
        functools.partial(_pool_kernel, tile=tile),
        grid=(b, s // tile),
        in_specs=[
            pl.BlockSpec((None, tile, d), lambda i, j: (i, j, 0)),
            pl.BlockSpec(gains.shape, const),
            pl.BlockSpec(w.shape, lambda *_: (0, 0, 0)),
            pl.BlockSpec(scale.shape, const),
        ],
        out_specs=pl.BlockSpec((None, tile, d), lambda i, j: (i, j, 0)),
        out_shape=jax.ShapeDtypeStruct(x.shape, x.dtype),
        scratch_shapes=[pltpu.VMEM((POOL_HALO + tile, d), _F32)],
        compiler_params=pltpu.CompilerParams(
            dimension_semantics=("arbitrary", "arbitrary"), vmem_limit_bytes=VMEM_LIMIT_BYTES),
        name="pool_mixer",
    )(x, gains, w, scale)


def _mlp_kernel(x_ref, gain_ref, wup_ref, cw_ref, cb_ref, wdn_ref, o_ref, u_ref, a_ref, *, tile):
    s = pl.program_id(1)
    x = x_ref[...]
    h = _rms(x, gain_ref[2:3, :]).astype(_BF16)

    @pl.when(s == 0)
    def _():
        u_ref[0:CONV_HALO, :] = jnp.zeros((CONV_HALO, 2 * D_FF), _F32)

    def conv(col):
        cols = slice(col, col + FF_CHUNK)
        u = _dot(h, wup_ref[:, cols])
        u_ref[CONV_HALO:CONV_HALO + tile, cols] = u
        um1 = u_ref[CONV_HALO - 1:CONV_HALO - 1 + tile, cols]
        um2 = u_ref[CONV_HALO - 2:CONV_HALO - 2 + tile, cols]
        return (cb_ref[:, cols] + um2 * cw_ref[0:1, cols] + um1 * cw_ref[1:2, cols]
                + u * cw_ref[2:3, cols])

    for c in range(D_FF // FF_CHUNK):
        gate = conv(c * FF_CHUNK)
        val = conv(D_FF + c * FF_CHUNK)
        a_ref[:, c * FF_CHUNK:(c + 1) * FF_CHUNK] = (_silu(gate) * val).astype(_BF16)

    u_ref[0:CONV_HALO, :] = u_ref[tile:tile + CONV_HALO, :]
    f = _dot(a_ref[...], wdn_ref[...])
    o_ref[...] = x + _rms(f, gain_ref[3:4, :])


def _mlp_layer(x, gains, w_up, conv_w, conv_b, w_down):
    b, s, d = x.shape
    tile = SEQ_TILE
    const = lambda *_: (0, 0)
    resident = functools.partial(pl.BlockSpec, index_map=const, pipeline_mode=pl.Buffered(1))
    return pl.pallas_call(
        functools.partial(_mlp_kernel, tile=tile),
        grid=(b, s // tile),
        in_specs=[
            pl.BlockSpec((None, tile, d), lambda i, j: (i, j, 0)),
            pl.BlockSpec(gains.shape, const),
            resident(w_up.shape),
            pl.BlockSpec(conv_w.shape, const),
            pl.BlockSpec(conv_b.shape, const),
            resident(w_down.shape),
        ],
        out_specs=pl.BlockSpec((None, tile, d), lambda i, j: (i, j, 0)),
        out_shape=jax.ShapeDtypeStruct(x.shape, x.dtype),
        scratch_shapes=[
            pltpu.VMEM((CONV_HALO + tile, 2 * D_FF), _F32),
            pltpu.VMEM((tile, D_FF), _BF16),
        ],
        compiler_params=pltpu.CompilerParams(
            dimension_semantics=("arbitrary", "arbitrary"), vmem_limit_bytes=VMEM_LIMIT_BYTES),
        name="conv_glu_mlp",
    )(x, gains, w_up, conv_w, conv_b, w_down)


def _ret_kernel(x_ref, pos_ref, gain_ref, freq_ref, win_ref, intra_ref, cross_ref, indec_ref,
                cdec_ref, gn_ref, wout_ref, o_ref, state_ref, y_ref, *, tile):
    s = pl.program_id(1)
    x = x_ref[...]
    h = _rms(x, gain_ref[0:1, :]).astype(_BF16)

    @pl.when(s == 0)
    def _():
        state_ref[...] = jnp.zeros(state_ref.shape, _F32)

    ang = pos_ref[...].astype(_F32) * freq_ref[...]
    cos, sin = jnp.cos(ang), jnp.sin(ang)
    q_scale = RET_QK_DIM ** -0.5
    half = RET_QK_DIM // 2

    def rotary(a, c, sn):
        a1, a2 = a[:, :half], a[:, half:]
        return jnp.concatenate([a1 * c - a2 * sn, a2 * c + a1 * sn], axis=-1)

    for hd in range(RET_HEADS):
        qc = slice(hd * RET_QK_DIM, (hd + 1) * RET_QK_DIM)
        kc = slice(RET_QK_TOTAL + hd * RET_QK_DIM, RET_QK_TOTAL + (hd + 1) * RET_QK_DIM)
        vc = slice(2 * RET_QK_TOTAL + hd * RET_V_DIM, 2 * RET_QK_TOTAL + (hd + 1) * RET_V_DIM)
        gc = slice(2 * RET_QK_TOTAL + RET_V_TOTAL + hd * RET_V_DIM,
                   2 * RET_QK_TOTAL + RET_V_TOTAL + (hd + 1) * RET_V_DIM)
        yc = slice(hd * RET_V_DIM, (hd + 1) * RET_V_DIM)
        q = rotary(_dot(h, win_ref[:, qc]), cos * q_scale, sin * q_scale)
        k = rotary(_dot(h, win_ref[:, kc]), cos, sin)
        v = _dot(h, win_ref[:, vc]).astype(_BF16)
        gate = _dot(h, win_ref[:, gc])
        intra = intra_ref[hd]
        cross = cross_ref[hd][:, 0:1]
        indec = indec_ref[hd][:, 0:1]
        cdec = cdec_ref[hd][0:1, 0:1]
        for sub in range(tile // RET_CHUNK):
            rows = slice(sub * RET_CHUNK, (sub + 1) * RET_CHUNK)
            qs = q[rows].astype(_BF16)
            ks = k[rows]
            vs = v[rows]
            state = state_ref[hd]
            scores = lax.dot_general(qs, ks.astype(_BF16), (((1,), (1,)), ((), ())),
                                     preferred_element_type=_F32) * intra
            o = _dot(scores.astype(_BF16), vs) + _dot(qs, state.astype(_BF16)) * cross
            kd = (ks * indec).astype(_BF16)
            state_ref[hd] = state * cdec + lax.dot_general(
                kd, vs, (((0,), (0,)), ((), ())), preferred_element_type=_F32)
            mu = jnp.mean(o, axis=-1, keepdims=True)
            dev = o - mu
            var = jnp.mean(dev * dev, axis=-1, keepdims=True)
            on = dev * lax.rsqrt(var + EPS) * gn_ref[:, yc]
            y_ref[rows, yc] = (_silu(gate[rows]) * on).astype(_BF16)

    f = _dot(y_ref[...], wout_ref[...])
    o_ref[...] = x + _rms(f, gain_ref[1:2, :])


def _retention_tables():
    log_gamma = jnp.log(1.0 - 2.0 ** (-5.0 - jnp.arange(RET_HEADS, dtype=_F32)))
    i = jnp.arange(RET_CHUNK, dtype=_F32)
    rel = i[:, None] - i[None, :]
    intra = jnp.where(rel >= 0, jnp.exp(jnp.maximum(rel, 0.0) * log_gamma[:, None, None]), 0.0)
    cross = jnp.exp((i + 1.0) * log_gamma[:, None])
    indec = jnp.exp((RET_CHUNK - 1.0 - i) * log_gamma[:, None])
    cdec = jnp.exp(RET_CHUNK * log_gamma)
    lanes = 128
    rep = lambda a: jnp.broadcast_to(a[:, :, None], a.shape + (lanes,))
    cdec = jnp.broadcast_to(cdec[:, None, None], (RET_HEADS, 8, lanes))
    inv_freq = ROPE_BASE ** (-jnp.arange(0, RET_QK_DIM, 2, dtype=_F32) / RET_QK_DIM)
    return inv_freq[None, :], intra, rep(cross), rep(indec), cdec


def _retention_layer(x, positions, gains, w_in, gn_gain, w_out):
    b, s, d = x.shape
    tile = SEQ_TILE
    inv_freq, intra, cross, indec, cdec = _retention_tables()
    const2 = lambda *_: (0, 0)
    const3 = lambda *_: (0, 0, 0)
    resident = functools.partial(pl.BlockSpec, index_map=const2, pipeline_mode=pl.Buffered(1))
    return pl.pallas_call(
        functools.partial(_ret_kernel, tile=tile),
        grid=(b, s // tile),
        in_specs=[
            pl.BlockSpec((None, tile, d), lambda i, j: (i, j, 0)),
            pl.BlockSpec((None, tile, 1), lambda i, j: (i, j, 0)),
            pl.BlockSpec(gains.shape, const2),
            pl.BlockSpec(inv_freq.shape, const2),
            resident(w_in.shape),
            pl.BlockSpec(intra.shape, const3),
            pl.BlockSpec(cross.shape, const3),
            pl.BlockSpec(indec.shape, const3),
            pl.BlockSpec(cdec.shape, const3),
            pl.BlockSpec(gn_gain.shape, const2),
            resident(w_out.shape),
        ],
        out_specs=pl.BlockSpec((None, tile, d), lambda i, j: (i, j, 0)),
        out_shape=jax.ShapeDtypeStruct(x.shape, x.dtype),
        scratch_shapes=[
            pltpu.VMEM((RET_HEADS, RET_QK_DIM, RET_V_DIM), _F32),
            pltpu.VMEM((tile, RET_V_TOTAL), _BF16),
        ],
        compiler_params=pltpu.CompilerParams(
            dimension_semantics=("arbitrary", "arbitrary"), vmem_limit_bytes=VMEM_LIMIT_BYTES),
        name="retention",
    )(x, positions[:, :, None], gains, inv_freq, w_in, intra, cross, indec, cdec, gn_gain, w_out)


def kernel(x, positions, norm_gain, pool_w, pool_scale, ret_w_in, ret_gn_gain, ret_w_out,
           mlp_w_up, mlp_conv_w, mlp_conv_b, mlp_w_down):
    bf = lambda a: a.astype(_BF16)
    x = _pool_layer(x, norm_gain[0], bf(pool_w[0]), pool_scale[0][None, :])
    x = _mlp_layer(x, norm_gain[0], bf(mlp_w_up[0]), mlp_conv_w[0], mlp_conv_b[0][None, :],
                   bf(mlp_w_down[0]))
    x = _retention_layer(x, positions, norm_gain[1], bf(ret_w_in[0]), ret_gn_gain[0][None, :],
                         bf(ret_w_out[0]))
    x = _mlp_layer(x, norm_gain[1], bf(mlp_w_up[1]), mlp_conv_w[1], mlp_conv_b[1][None, :],
                   bf(mlp_w_down[1]))
    return x
```

```python
import functools

import jax
import jax.numpy as jnp
from jax import lax
from jax.experimental import pallas as pl
from jax.experimental.pallas import tpu as pltpu

LANES = 128
D_MODEL = 1024
POOL_WINDOWS = (2, 4, 8, 16)
POOL_GROUP_DIM = D_MODEL // len(POOL_WINDOWS)
POOL_HALO = 16
RET_HEADS = 4
RET_QK_DIM = 256
RET_V_DIM = 512
RET_QK_TOTAL = RET_HEADS * RET_QK_DIM
RET_V_TOTAL = RET_HEADS * RET_V_DIM
RET_CHUNK = 256
ROPE_BASE = 10000.0
D_FF = 2816
FF_CHUNK = 256
UP_ROWS = 256
GATE_LEAD = 6
CONV_HALO = 8
EPS = 1e-6

SEQ_TILE = 512
VMEM_LIMIT_BYTES = 56 * 1024 * 1024

_F32 = jnp.float32
_BF16 = jnp.bfloat16


def _dot(a, b):
    return jnp.dot(a, b, preferred_element_type=_F32)


def _rms(x, gain):
    var = jnp.mean(x * x, axis=-1, keepdims=True)
    return x * lax.rsqrt(var + EPS) * gain


def _silu(x):
    return x * (1.0 / (1.0 + jnp.exp(-x)))


def _pool_kernel(x_ref, gain_ref, w_ref, scale_ref, o_ref, ext_ref, *, tile):
    s = pl.program_id(1)
    x = x_ref[...]
    h = _rms(x, gain_ref[0:1, :])

    @pl.when(s == 0)
    def _():
        ext_ref[:, 0:POOL_HALO, :] = jnp.zeros((D_MODEL // LANES, POOL_HALO, LANES), _F32)

    t = s * tile + lax.broadcasted_iota(jnp.int32, (tile, 1), 0)
    parts = []
    for g, win in enumerate(POOL_WINDOWS):
        inv_count = 1.0 / jnp.minimum(t + 1, win).astype(_F32)
        diffs = []
        for k in range(POOL_GROUP_DIM // LANES):
            lo = g * POOL_GROUP_DIM + k * LANES
            slab = ext_ref.at[lo // LANES]
            hk = h[:, lo:lo + LANES]
            slab[POOL_HALO:POOL_HALO + tile, :] = hk
            acc = hk
            for j in range(1, win):
                acc = acc + slab[POOL_HALO - j:POOL_HALO - j + tile, :]
            slab[0:POOL_HALO, :] = hk[tile - POOL_HALO:, :]
            diffs.append(acc * inv_count - hk)
        diff = jnp.concatenate(diffs, axis=-1)
        parts.append(_dot(diff.astype(_BF16), w_ref[g]))
    y = jnp.concatenate(parts, axis=-1) * scale_ref[...]
    o_ref[...] = x + _rms(y, gain_ref[1:2, :])


def _pool_layer(x, gains, w, scale):
    b, s, d = x.shape
    tile = SEQ_TILE
    const = lambda *_: (0, 0)
    return pl.pallas_call(
        functools.partial(_pool_kernel, tile=tile),
        grid=(b, s // tile),
        in_specs=[
            pl.BlockSpec((None, tile, d), lambda i, j: (i, j, 0)),
            pl.BlockSpec(gains.shape, const),
            pl.BlockSpec(w.shape, lambda *_: (0, 0, 0)),
            pl.BlockSpec(scale.shape, const),
        ],
        out_specs=pl.BlockSpec((None, tile, d), lambda i, j: (i, j, 0)),
        out_shape=jax.ShapeDtypeStruct(x.shape, x.dtype),
        scratch_shapes=[pltpu.VMEM((d // LANES, POOL_HALO + tile, LANES), _F32)],
        compiler_params=pltpu.CompilerParams(
            dimension_semantics=("arbitrary", "arbitrary"), vmem_limit_bytes=VMEM_LIMIT_BYTES),
        name="pool_mixer",
    )(x, gains, w, scale)


def _mlp_kernel(xc_ref, xp_ref, gain_ref, wup_ref, cw_ref, cb_ref, wdn_ref, o_ref, u_ref, a_ref,
                *, tile, tiles_per_row):
    i = pl.program_id(0)

    @pl.when(i == 0)
    def _():
        a_ref[...] = jnp.zeros(a_ref.shape, _BF16)
        u_ref[...] = jnp.zeros(u_ref.shape, _F32)

    @pl.when((i - 1) % tiles_per_row == 0)
    def _():
        u_ref[:, 0:CONV_HALO, :] = jnp.zeros((2 * D_FF // LANES, CONV_HALO, LANES), _F32)

    a_ref[0] = a_ref[1]
    f = _dot(a_ref[0], wdn_ref[...])
    o_ref[...] = xp_ref[...] + _rms(f, gain_ref[3:4, :])

    h = _rms(xc_ref[...], gain_ref[2:3, :]).astype(_BF16)

    def project(c):
        for r0 in range(0, tile, UP_ROWS):
            for col in (c * FF_CHUNK, D_FF + c * FF_CHUNK):
                u = _dot(h[r0:r0 + UP_ROWS], wup_ref[:, col:col + FF_CHUNK])
                for k in range(FF_CHUNK // LANES):
                    slab = u_ref.at[col // LANES + k]
                    slab[CONV_HALO + r0:CONV_HALO + r0 + UP_ROWS, :] = u[:, k * LANES:(k + 1) * LANES]

    def conv(col):
        parts = []
        for k in range(FF_CHUNK // LANES):
            cols = slice(col + k * LANES, col + (k + 1) * LANES)
            slab = u_ref.at[cols.start // LANES]
            taps = [slab[CONV_HALO - 2 + j:CONV_HALO - 2 + j + tile, :] * cw_ref[j:j + 1, cols]
                    for j in range(3)]
            parts.append(cb_ref[:, cols] + taps[0] + taps[1] + taps[2])
            slab[0:CONV_HALO, :] = slab[tile:tile + CONV_HALO, :]
        return jnp.concatenate(parts, axis=-1)

    def gated(c):
        gate = conv(c * FF_CHUNK)
        val = conv(D_FF + c * FF_CHUNK)
        a_ref[1, :, c * FF_CHUNK:(c + 1) * FF_CHUNK] = (_silu(gate) * val).astype(_BF16)

    n_chunks = D_FF // FF_CHUNK
    for c in range(GATE_LEAD):
        gated(c)
    for c in range(n_chunks):
        project(c)
        if c + GATE_LEAD < n_chunks:
            gated(c + GATE_LEAD)


def _mlp_layer(x, gains, layer, w_up, conv_w, conv_b, w_down):
    b, s, d = x.shape
    tile = SEQ_TILE
    n_tiles = b * s // tile
    const = lambda *_: (0, 0)
    resident = lambda a: pl.BlockSpec((None,) + a.shape[1:], lambda *_: (layer, 0, 0),
                                      pipeline_mode=pl.Buffered(1))
    cur = lambda i: (jnp.minimum(i, n_tiles - 1), 0)
    done = lambda i: (jnp.maximum(i - 2, 0), 0)
    x2 = x.reshape(b * s, d)
    out = pl.pallas_call(
        functools.partial(_mlp_kernel, tile=tile, tiles_per_row=s // tile),
        grid=(n_tiles + 2,),
        in_specs=[
            pl.BlockSpec((tile, d), cur),
            pl.BlockSpec((tile, d), done),
            pl.BlockSpec(gains.shape, const),
            resident(w_up),
            pl.BlockSpec(conv_w.shape, const),
            pl.BlockSpec(conv_b.shape, const),
            resident(w_down),
        ],
        out_specs=pl.BlockSpec((tile, d), done),
        out_shape=jax.ShapeDtypeStruct(x2.shape, x.dtype),
        scratch_shapes=[
            pltpu.VMEM((2 * D_FF // LANES, CONV_HALO + tile, LANES), _F32),
            pltpu.VMEM((2, tile, D_FF), _BF16),
        ],
        compiler_params=pltpu.CompilerParams(
            dimension_semantics=("arbitrary",), vmem_limit_bytes=VMEM_LIMIT_BYTES),
        name="conv_glu_mlp",
    )(x2, x2, gains, w_up, conv_w, conv_b, w_down)
    return out.reshape(b, s, d)


def _ret_kernel(x_ref, pos_ref, gain_ref, freq_ref, win_ref, intra_ref, cross_ref, indec_ref,
                cdec_ref, gn_ref, wout_ref, o_ref, state_ref, y_ref, *, tile):
    s = pl.program_id(1)
    x = x_ref[...]
    h = _rms(x, gain_ref[0:1, :]).astype(_BF16)

    @pl.when(s == 0)
    def _():
        state_ref[...] = jnp.zeros(state_ref.shape, _F32)

    ang = pos_ref[...].astype(_F32) * freq_ref[...]
    cos, sin = jnp.cos(ang), jnp.sin(ang)
    q_scale = RET_QK_DIM ** -0.5
    half = RET_QK_DIM // 2

    def rotary(a, c, sn):
        a1, a2 = a[:, :half], a[:, half:]
        return jnp.concatenate([a1 * c - a2 * sn, a2 * c + a1 * sn], axis=-1)

    for hd in range(RET_HEADS):
        qc = slice(hd * RET_QK_DIM, (hd + 1) * RET_QK_DIM)
        kc = slice(RET_QK_TOTAL + hd * RET_QK_DIM, RET_QK_TOTAL + (hd + 1) * RET_QK_DIM)
        vc = slice(2 * RET_QK_TOTAL + hd * RET_V_DIM, 2 * RET_QK_TOTAL + (hd + 1) * RET_V_DIM)
        gc = slice(2 * RET_QK_TOTAL + RET_V_TOTAL + hd * RET_V_DIM,
                   2 * RET_QK_TOTAL + RET_V_TOTAL + (hd + 1) * RET_V_DIM)
        yc = slice(hd * RET_V_DIM, (hd + 1) * RET_V_DIM)
        q = rotary(_dot(h, win_ref[:, qc]), cos * q_scale, sin * q_scale)
        k = rotary(_dot(h, win_ref[:, kc]), cos, sin)
        v = _dot(h, win_ref[:, vc]).astype(_BF16)
        gate = _dot(h, win_ref[:, gc])
        intra = intra_ref[hd]
        cross = cross_ref[hd][:, 0:1]
        indec = indec_ref[hd][:, 0:1]
        cdec = cdec_ref[hd][0:1, 0:1]
        for sub in range(tile // RET_CHUNK):
            rows = slice(sub * RET_CHUNK, (sub + 1) * RET_CHUNK)
            qs = q[rows].astype(_BF16)
            ks = k[rows]
            vs = v[rows]
            state = state_ref[hd]
            scores = lax.dot_general(qs, ks.astype(_BF16), (((1,), (1,)), ((), ())),
                                     preferred_element_type=_F32) * intra
            o = _dot(scores.astype(_BF16), vs) + _dot(qs, state.astype(_BF16)) * cross
            kd = (ks * indec).astype(_BF16)
            state_ref[hd] = state * cdec + lax.dot_general(
                kd, vs, (((0,), (0,)), ((), ())), preferred_element_type=_F32)
            mu = jnp.mean(o, axis=-1, keepdims=True)
            dev = o - mu
            var = jnp.mean(dev * dev, axis=-1, keepdims=True)
            on = dev * lax.rsqrt(var + EPS) * gn_ref[:, yc]
            y_ref[rows, yc] = (_silu(gate[rows]) * on).astype(_BF16)

    f = _dot(y_ref[...], wout_ref[...])
    o_ref[...] = x + _rms(f, gain_ref[1:2, :])


def _retention_tables():
    log_gamma = jnp.log(1.0 - 2.0 ** (-5.0 - jnp.arange(RET_HEADS, dtype=_F32)))
    i = jnp.arange(RET_CHUNK, dtype=_F32)
    rel = i[:, None] - i[None, :]
    intra = jnp.where(rel >= 0, jnp.exp(jnp.maximum(rel, 0.0) * log_gamma[:, None, None]), 0.0)
    cross = jnp.exp((i + 1.0) * log_gamma[:, None])
    indec = jnp.exp((RET_CHUNK - 1.0 - i) * log_gamma[:, None])
    cdec = jnp.exp(RET_CHUNK * log_gamma)
    lanes = 128
    rep = lambda a: jnp.broadcast_to(a[:, :, None], a.shape + (lanes,))
    cdec = jnp.broadcast_to(cdec[:, None, None], (RET_HEADS, 8, lanes))
    inv_freq = ROPE_BASE ** (-jnp.arange(0, RET_QK_DIM, 2, dtype=_F32) / RET_QK_DIM)
    return inv_freq[None, :], intra, rep(cross), rep(indec), cdec


def _retention_layer(x, positions, gains, w_in, gn_gain, w_out):
    b, s, d = x.shape
    tile = SEQ_TILE
    inv_freq, intra, cross, indec, cdec = _retention_tables()
    const2 = lambda *_: (0, 0)
    const3 = lambda *_: (0, 0, 0)
    resident = functools.partial(pl.BlockSpec, index_map=const2, pipeline_mode=pl.Buffered(1))
    return pl.pallas_call(
        functools.partial(_ret_kernel, tile=tile),
        grid=(b, s // tile),
        in_specs=[
            pl.BlockSpec((None, tile, d), lambda i, j: (i, j, 0)),
            pl.BlockSpec((None, tile, 1), lambda i, j: (i, j, 0)),
            pl.BlockSpec(gains.shape, const2),
            pl.BlockSpec(inv_freq.shape, const2),
            resident(w_in.shape),
            pl.BlockSpec(intra.shape, const3),
            pl.BlockSpec(cross.shape, const3),
            pl.BlockSpec(indec.shape, const3),
            pl.BlockSpec(cdec.shape, const3),
            pl.BlockSpec(gn_gain.shape, const2),
            resident(w_out.shape),
        ],
        out_specs=pl.BlockSpec((None, tile, d), lambda i, j: (i, j, 0)),
        out_shape=jax.ShapeDtypeStruct(x.shape, x.dtype),
        scratch_shapes=[
            pltpu.VMEM((RET_HEADS, RET_QK_DIM, RET_V_DIM), _F32),
            pltpu.VMEM((tile, RET_V_TOTAL), _BF16),
        ],
        compiler_params=pltpu.CompilerParams(
            dimension_semantics=("arbitrary", "arbitrary"), vmem_limit_bytes=VMEM_LIMIT_BYTES),
        name="retention",
    )(x, positions[:, :, None], gains, inv_freq, w_in, intra, cross, indec, cdec, gn_gain, w_out)


def kernel(x, positions, norm_gain, pool_w, pool_scale, ret_w_in, ret_gn_gain, ret_w_out,
           mlp_w_up, mlp_conv_w, mlp_conv_b, mlp_w_down):
    bf = lambda a: a.astype(_BF16)
    w_up, w_down = bf(mlp_w_up), bf(mlp_w_down)
    x = _pool_layer(x, norm_gain[0], bf(pool_w[0]), pool_scale[0][None, :])
    x = _mlp_layer(x, norm_gain[0], 0, w_up, mlp_conv_w[0], mlp_conv_b[0][None, :], w_down)
    x = _retention_layer(x, positions, norm_gain[1], bf(ret_w_in[0]), ret_gn_gain[0][None, :],
                         bf(ret_w_out[0]))
    x = _mlp_layer(x, norm_gain[1], 1, w_up, mlp_conv_w[1], mlp_conv_b[1][None, :], w_down)
    return x
```

```python
import functools

import jax
import jax.numpy as jnp
from jax import lax
from jax.experimental import pallas as pl
from jax.experimental.pallas import tpu as pltpu

LANES = 128
D_MODEL = 1024
POOL_WINDOWS = (2, 4, 8, 16)
POOL_GROUP_DIM = D_MODEL // len(POOL_WINDOWS)
POOL_HALO = 16
RET_HEADS = 4
RET_QK_DIM = 256
RET_V_DIM = 512
RET_QK_TOTAL = RET_HEADS * RET_QK_DIM
RET_V_TOTAL = RET_HEADS * RET_V_DIM
RET_CHUNK = 256
ROPE_BASE = 10000.0
D_FF = 2816
FF_CHUNK = 256
UP_ROWS = 256
GATE_LEAD = 6
CONV_HALO = 8
EPS = 1e-6

SEQ_TILE = 512
VMEM_LIMIT_BYTES = 56 * 1024 * 1024

_F32 = jnp.float32
_BF16 = jnp.bfloat16


def _dot(a, b):
    return jnp.dot(a, b, preferred_element_type=_F32)


def _rms(x, gain):
    var = jnp.mean(x * x, axis=-1, keepdims=True)
    return x * lax.rsqrt(var + EPS) * gain


def _silu(x):
    return x * (1.0 / (1.0 + jnp.exp(-x)))


def _pool_kernel(x_ref, gain_ref, w_ref, scale_ref, o_ref, ext_ref, *, tile):
    s = pl.program_id(1)
    x = x_ref[...]
    h = _rms(x, gain_ref[0:1, :])

    @pl.when(s == 0)
    def _():
        ext_ref[:, 0:POOL_HALO, :] = jnp.zeros((D_MODEL // LANES, POOL_HALO, LANES), _F32)

    t = s * tile + lax.broadcasted_iota(jnp.int32, (tile, 1), 0)
    parts = []
    for g, win in enumerate(POOL_WINDOWS):
        inv_count = 1.0 / jnp.minimum(t + 1, win).astype(_F32)
        diffs = []
        for k in range(POOL_GROUP_DIM // LANES):
            lo = g * POOL_GROUP_DIM + k * LANES
            slab = ext_ref.at[lo // LANES]
            hk = h[:, lo:lo + LANES]
            slab[POOL_HALO:POOL_HALO + tile, :] = hk
            acc = hk
            for j in range(1, win):
                acc = acc + slab[POOL_HALO - j:POOL_HALO - j + tile, :]
            slab[0:POOL_HALO, :] = hk[tile - POOL_HALO:, :]
            diffs.append(acc * inv_count - hk)
        diff = jnp.concatenate(diffs, axis=-1)
        parts.append(_dot(diff.astype(_BF16), w_ref[g]))
    y = jnp.concatenate(parts, axis=-1) * scale_ref[...]
    o_ref[...] = x + _rms(y, gain_ref[1:2, :])


def _pool_layer(x, gains, w, scale):
    b, s, d = x.shape
    tile = SEQ_TILE
    const = lambda *_: (0, 0)
    return pl.pallas_call(
        functools.partial(_pool_kernel, tile=tile),
        grid=(b, s // tile),
        in_specs=[
            pl.BlockSpec((None, tile, d), lambda i, j: (i, j, 0)),
            pl.BlockSpec(gains.shape, const),
            pl.BlockSpec(w.shape, lambda *_: (0, 0, 0)),
            pl.BlockSpec(scale.shape, const),
        ],
        out_specs=pl.BlockSpec((None, tile, d), lambda i, j: (i, j, 0)),
        out_shape=jax.ShapeDtypeStruct(x.shape, x.dtype),
        scratch_shapes=[pltpu.VMEM((d // LANES, POOL_HALO + tile, LANES), _F32)],
        compiler_params=pltpu.CompilerParams(
            dimension_semantics=("arbitrary", "arbitrary"), vmem_limit_bytes=VMEM_LIMIT_BYTES),
        name="pool_mixer",
    )(x, gains, w, scale)


def _mlp_kernel(xc_ref, xp_ref, gain_ref, wup_ref, cw_ref, cb_ref, wdn_ref, o_ref, u_ref, a_ref,
                *, tile, tiles_per_row):
    i = pl.program_id(0)

    @pl.when(i == 0)
    def _():
        a_ref[...] = jnp.zeros(a_ref.shape, _BF16)
        u_ref[...] = jnp.zeros(u_ref.shape, _F32)

    @pl.when((i - 1) % tiles_per_row == 0)
    def _():
        u_ref[:, 0:CONV_HALO, :] = jnp.zeros((2 * D_FF // LANES, CONV_HALO, LANES), _F32)

    a_ref[0] = a_ref[1]
    f = _dot(a_ref[0], wdn_ref[...])
    o_ref[...] = xp_ref[...] + _rms(f, gain_ref[3:4, :])

    h = _rms(xc_ref[...], gain_ref[2:3, :]).astype(_BF16)

    def project(c):
        for r0 in range(0, tile, UP_ROWS):
            for col in (c * FF_CHUNK, D_FF + c * FF_CHUNK):
                u = _dot(h[r0:r0 + UP_ROWS], wup_ref[:, col:col + FF_CHUNK])
                for k in range(FF_CHUNK // LANES):
                    slab = u_ref.at[col // LANES + k]
                    slab[CONV_HALO + r0:CONV_HALO + r0 + UP_ROWS, :] = u[:, k * LANES:(k + 1) * LANES]

    def conv(col):
        parts = []
        for k in range(FF_CHUNK // LANES):
            cols = slice(col + k * LANES, col + (k + 1) * LANES)
            slab = u_ref.at[cols.start // LANES]
            taps = [slab[CONV_HALO - 2 + j:CONV_HALO - 2 + j + tile, :] * cw_ref[j:j + 1, cols]
                    for j in range(3)]
            parts.append(cb_ref[:, cols] + taps[0] + taps[1] + taps[2])
            slab[0:CONV_HALO, :] = slab[tile:tile + CONV_HALO, :]
        return jnp.concatenate(parts, axis=-1)

    def gated(c):
        gate = conv(c * FF_CHUNK)
        val = conv(D_FF + c * FF_CHUNK)
        a_ref[1, :, c * FF_CHUNK:(c + 1) * FF_CHUNK] = (_silu(gate) * val).astype(_BF16)

    n_chunks = D_FF // FF_CHUNK
    for c in range(GATE_LEAD):
        gated(c)
    for c in range(n_chunks):
        project(c)
        if c + GATE_LEAD < n_chunks:
            gated(c + GATE_LEAD)


def _mlp_layer(x, gains, layer, w_up, conv_w, conv_b, w_down):
    b, s, d = x.shape
    tile = SEQ_TILE
    n_tiles = b * s // tile
    const = lambda *_: (0, 0)
    resident = lambda a: pl.BlockSpec((None,) + a.shape[1:], lambda *_: (layer, 0, 0),
                                      pipeline_mode=pl.Buffered(1))
    cur = lambda i: (jnp.minimum(i, n_tiles - 1), 0)
    done = lambda i: (jnp.maximum(i - 2, 0), 0)
    x2 = x.reshape(b * s, d)
    out = pl.pallas_call(
        functools.partial(_mlp_kernel, tile=tile, tiles_per_row=s // tile),
        grid=(n_tiles + 2,),
        in_specs=[
            pl.BlockSpec((tile, d), cur),
            pl.BlockSpec((tile, d), done),
            pl.BlockSpec(gains.shape, const),
            resident(w_up),
            pl.BlockSpec(conv_w.shape, const),
            pl.BlockSpec(conv_b.shape, const),
            resident(w_down),
        ],
        out_specs=pl.BlockSpec((tile, d), done),
        out_shape=jax.ShapeDtypeStruct(x2.shape, x.dtype),
        scratch_shapes=[
            pltpu.VMEM((2 * D_FF // LANES, CONV_HALO + tile, LANES), _F32),
            pltpu.VMEM((2, tile, D_FF), _BF16),
        ],
        compiler_params=pltpu.CompilerParams(
            dimension_semantics=("arbitrary",), vmem_limit_bytes=VMEM_LIMIT_BYTES),
        name="conv_glu_mlp",
    )(x2, x2, gains, w_up, conv_w, conv_b, w_down)
    return out.reshape(b, s, d)


def _ret_kernel(xc_ref, pos_ref, xd_ref, gain_ref, freq_ref, win_ref, intra_ref, cross_ref,
                indec_ref, cdec_ref, gn_ref, wout_ref, o_ref, state_ref, qkh_ref, y_ref,
                *, tile, tiles_per_row):
    i = pl.program_id(0)
    n_sub = tile // RET_CHUNK
    q0, k0, kd0, h0 = 0, RET_QK_TOTAL, 2 * RET_QK_TOTAL, 3 * RET_QK_TOTAL

    @pl.when(i == 0)
    def _():
        qkh_ref[...] = jnp.zeros(qkh_ref.shape, _BF16)
        y_ref[...] = jnp.zeros(y_ref.shape, _BF16)

    @pl.when(jnp.logical_or(i == 0, (i - 1) % tiles_per_row == 0))
    def _():
        state_ref[...] = jnp.zeros(state_ref.shape, _F32)

    qkh_ref[0] = qkh_ref[1]
    y_ref[0] = y_ref[1]

    f = _dot(y_ref[0], wout_ref[...])
    o_ref[...] = xd_ref[...] + _rms(f, gain_ref[1:2, :])

    hp = qkh_ref[0, :, h0:h0 + D_MODEL]
    for hd in range(RET_HEADS):
        vc = slice(2 * RET_QK_TOTAL + hd * RET_V_DIM, 2 * RET_QK_TOTAL + (hd + 1) * RET_V_DIM)
        gc = slice(2 * RET_QK_TOTAL + RET_V_TOTAL + hd * RET_V_DIM,
                   2 * RET_QK_TOTAL + RET_V_TOTAL + (hd + 1) * RET_V_DIM)
        yc = slice(hd * RET_V_DIM, (hd + 1) * RET_V_DIM)
        hc = slice(hd * RET_QK_DIM, (hd + 1) * RET_QK_DIM)
        v = _dot(hp, win_ref[:, vc]).astype(_BF16)
        gate = _dot(hp, win_ref[:, gc])
        intra = intra_ref[hd]
        cross = cross_ref[hd][:, 0:1]
        cdec = cdec_ref[hd][0:1, 0:1]
        for sub in range(n_sub):
            rows = slice(sub * RET_CHUNK, (sub + 1) * RET_CHUNK)
            qs = qkh_ref[0, rows, q0 + hc.start:q0 + hc.stop]
            ks = qkh_ref[0, rows, k0 + hc.start:k0 + hc.stop]
            kd = qkh_ref[0, rows, kd0 + hc.start:kd0 + hc.stop]
            vs = v[rows]
            state = state_ref[hd]
            scores = lax.dot_general(qs, ks, (((1,), (1,)), ((), ())),
                                     preferred_element_type=_F32) * intra
            o = _dot(scores.astype(_BF16), vs) + _dot(qs, state.astype(_BF16)) * cross
            state_ref[hd] = state * cdec + lax.dot_general(
                kd, vs, (((0,), (0,)), ((), ())), preferred_element_type=_F32)
            mu = jnp.mean(o, axis=-1, keepdims=True)
            dev = o - mu
            var = jnp.mean(dev * dev, axis=-1, keepdims=True)
            on = dev * lax.rsqrt(var + EPS) * gn_ref[:, yc]
            y_ref[1, rows, yc] = (_silu(gate[rows]) * on).astype(_BF16)

    h = _rms(xc_ref[...], gain_ref[0:1, :]).astype(_BF16)
    qkh_ref[1, :, h0:h0 + D_MODEL] = h
    ang = pos_ref[...].astype(_F32) * freq_ref[...]
    cos, sin = jnp.cos(ang), jnp.sin(ang)
    q_scale = RET_QK_DIM ** -0.5
    half = RET_QK_DIM // 2

    def rotary(a, c, sn):
        a1, a2 = a[:, :half], a[:, half:]
        return jnp.concatenate([a1 * c - a2 * sn, a2 * c + a1 * sn], axis=-1)

    for hd in range(RET_HEADS):
        hc = slice(hd * RET_QK_DIM, (hd + 1) * RET_QK_DIM)
        q = rotary(_dot(h, win_ref[:, q0 + hc.start:q0 + hc.stop]), cos * q_scale, sin * q_scale)
        k = rotary(_dot(h, win_ref[:, k0 + hc.start:k0 + hc.stop]), cos, sin)
        qkh_ref[1, :, q0 + hc.start:q0 + hc.stop] = q.astype(_BF16)
        qkh_ref[1, :, k0 + hc.start:k0 + hc.stop] = k.astype(_BF16)
        indec = indec_ref[hd][:, 0:1]
        for sub in range(n_sub):
            rows = slice(sub * RET_CHUNK, (sub + 1) * RET_CHUNK)
            qkh_ref[1, rows, kd0 + hc.start:kd0 + hc.stop] = (k[rows] * indec).astype(_BF16)


def _retention_tables():
    log_gamma = jnp.log(1.0 - 2.0 ** (-5.0 - jnp.arange(RET_HEADS, dtype=_F32)))
    i = jnp.arange(RET_CHUNK, dtype=_F32)
    rel = i[:, None] - i[None, :]
    intra = jnp.where(rel >= 0, jnp.exp(jnp.maximum(rel, 0.0) * log_gamma[:, None, None]), 0.0)
    cross = jnp.exp((i + 1.0) * log_gamma[:, None])
    indec = jnp.exp((RET_CHUNK - 1.0 - i) * log_gamma[:, None])
    cdec = jnp.exp(RET_CHUNK * log_gamma)
    lanes = 128
    rep = lambda a: jnp.broadcast_to(a[:, :, None], a.shape + (lanes,))
    cdec = jnp.broadcast_to(cdec[:, None, None], (RET_HEADS, 8, lanes))
    inv_freq = ROPE_BASE ** (-jnp.arange(0, RET_QK_DIM, 2, dtype=_F32) / RET_QK_DIM)
    return inv_freq[None, :], intra, rep(cross), rep(indec), cdec


def _retention_layer(x, positions, gains, w_in, gn_gain, w_out):
    b, s, d = x.shape
    tile = SEQ_TILE
    n_tiles = b * s // tile
    inv_freq, intra, cross, indec, cdec = _retention_tables()
    const2 = lambda *_: (0, 0)
    const3 = lambda *_: (0, 0, 0)
    once2 = functools.partial(pl.BlockSpec, index_map=const2, pipeline_mode=pl.Buffered(1))
    once3 = functools.partial(pl.BlockSpec, index_map=const3, pipeline_mode=pl.Buffered(1))
    cur = lambda i: (jnp.minimum(i, n_tiles - 1), 0)
    done = lambda i: (jnp.maximum(i - 2, 0), 0)
    x2 = x.reshape(b * s, d)
    out = pl.pallas_call(
        functools.partial(_ret_kernel, tile=tile, tiles_per_row=s // tile),
        grid=(n_tiles + 2,),
        in_specs=[
            pl.BlockSpec((tile, d), cur),
            pl.BlockSpec((tile, 1), cur),
            pl.BlockSpec((tile, d), done),
            pl.BlockSpec(gains.shape, const2),
            pl.BlockSpec(inv_freq.shape, const2),
            once2(w_in.shape),
            once3(intra.shape),
            once3(cross.shape),
            once3(indec.shape),
            once3(cdec.shape),
            pl.BlockSpec(gn_gain.shape, const2),
            once2(w_out.shape),
        ],
        out_specs=pl.BlockSpec((tile, d), done),
        out_shape=jax.ShapeDtypeStruct(x2.shape, x.dtype),
        scratch_shapes=[
            pltpu.VMEM((RET_HEADS, RET_QK_DIM, RET_V_DIM), _F32),
            pltpu.VMEM((2, tile, 3 * RET_QK_TOTAL + D_MODEL), _BF16),
            pltpu.VMEM((2, tile, RET_V_TOTAL), _BF16),
        ],
        compiler_params=pltpu.CompilerParams(
            dimension_semantics=("arbitrary",), vmem_limit_bytes=VMEM_LIMIT_BYTES),
        name="retention",
    )(x2, positions.reshape(b * s, 1), x2, gains, inv_freq, w_in, intra, cross, indec, cdec,
      gn_gain, w_out)
    return out.reshape(b, s, d)


def kernel(x, positions, norm_gain, pool_w, pool_scale, ret_w_in, ret_gn_gain, ret_w_out,
           mlp_w_up, mlp_conv_w, mlp_conv_b, mlp_w_down):
    bf = lambda a: a.astype(_BF16)
    w_up, w_down = bf(mlp_w_up), bf(mlp_w_down)
    x = _pool_layer(x, norm_gain[0], bf(pool_w[0]), pool_scale[0][None, :])
    x = _mlp_layer(x, norm_gain[0], 0, w_up, mlp_conv_w[0], mlp_conv_b[0][None, :], w_down)
    x = _retention_layer(x, positions, norm_gain[1], bf(ret_w_in[0]), ret_gn_gain[0][None, :],
                         bf(ret_w_out[0]))
    x = _mlp_layer(x, norm_gain[1], 1, w_up, mlp_conv_w[1], mlp_conv_b[1][None, :], w_down)
    return x
```

```python
import functools

import jax
import jax.numpy as jnp
from jax import lax
from jax.experimental import pallas as pl
from jax.experimental.pallas import tpu as pltpu

LANES = 128
D_MODEL = 1024
POOL_WINDOWS = (2, 4, 8, 16)
POOL_GROUP_DIM = D_MODEL // len(POOL_WINDOWS)
POOL_HALO = 16
RET_HEADS = 4
RET_QK_DIM = 256
RET_V_DIM = 512
RET_QK_TOTAL = RET_HEADS * RET_QK_DIM
RET_V_TOTAL = RET_HEADS * RET_V_DIM
RET_CHUNK = 256
ROPE_BASE = 10000.0
D_FF = 2816
FF_CHUNK = 256
UP_ROWS = 128
GATE_LEAD = 6
CONV_HALO = 8
EPS = 1e-6

SEQ_TILE = 512
VMEM_LIMIT_BYTES = 56 * 1024 * 1024

_F32 = jnp.float32
_BF16 = jnp.bfloat16


def _dot(a, b):
    return jnp.dot(a, b, preferred_element_type=_F32)


def _rms(x, gain):
    var = jnp.mean(x * x, axis=-1, keepdims=True)
    return x * lax.rsqrt(var + EPS) * gain


def _silu(x):
    return x * (1.0 / (1.0 + jnp.exp(-x)))


def _pool_kernel(x_ref, gain_ref, w_ref, scale_ref, o_ref, ext_ref, *, tile):
    s = pl.program_id(1)
    x = x_ref[...]
    h = _rms(x, gain_ref[0:1, :])

    @pl.when(s == 0)
    def _():
        ext_ref[:, 0:POOL_HALO, :] = jnp.zeros((D_MODEL // LANES, POOL_HALO, LANES), _F32)

    t = s * tile + lax.broadcasted_iota(jnp.int32, (tile, 1), 0)
    parts = []
    for g, win in enumerate(POOL_WINDOWS):
        inv_count = 1.0 / jnp.minimum(t + 1, win).astype(_F32)
        diffs = []
        for k in range(POOL_GROUP_DIM // LANES):
            lo = g * POOL_GROUP_DIM + k * LANES
            slab = ext_ref.at[lo // LANES]
            hk = h[:, lo:lo + LANES]
            slab[POOL_HALO:POOL_HALO + tile, :] = hk
            acc = hk
            for j in range(1, win):
                acc = acc + slab[POOL_HALO - j:POOL_HALO - j + tile, :]
            slab[0:POOL_HALO, :] = hk[tile - POOL_HALO:, :]
            diffs.append(acc * inv_count - hk)
        diff = jnp.concatenate(diffs, axis=-1)
        parts.append(_dot(diff.astype(_BF16), w_ref[g]))
    y = jnp.concatenate(parts, axis=-1) * scale_ref[...]
    o_ref[...] = x + _rms(y, gain_ref[1:2, :])


def _pool_layer(x, gains, w, scale):
    b, s, d = x.shape
    tile = SEQ_TILE
    const = lambda *_: (0, 0)
    return pl.pallas_call(
        functools.partial(_pool_kernel, tile=tile),
        grid=(b, s // tile),
        in_specs=[
            pl.BlockSpec((None, tile, d), lambda i, j: (i, j, 0)),
            pl.BlockSpec(gains.shape, const),
            pl.BlockSpec(w.shape, lambda *_: (0, 0, 0)),
            pl.BlockSpec(scale.shape, const),
        ],
        out_specs=pl.BlockSpec((None, tile, d), lambda i, j: (i, j, 0)),
        out_shape=jax.ShapeDtypeStruct(x.shape, x.dtype),
        scratch_shapes=[pltpu.VMEM((d // LANES, POOL_HALO + tile, LANES), _F32)],
        compiler_params=pltpu.CompilerParams(
            dimension_semantics=("arbitrary", "arbitrary"), vmem_limit_bytes=VMEM_LIMIT_BYTES),
        name="pool_mixer",
    )(x, gains, w, scale)


def _mlp_kernel(xc_ref, xp_ref, gain_ref, wup_ref, cw_ref, cb_ref, wdn_ref, o_ref, u_ref, a_ref,
                *, tile, tiles_per_row, n_tiles):
    i = pl.program_id(0)
    n_chunks = D_FF // FF_CHUNK

    @pl.when((i - 1) % tiles_per_row == 0)
    def _():
        u_ref[:, 0:CONV_HALO, :] = jnp.zeros((2 * D_FF // LANES, CONV_HALO, LANES), _F32)

    def down():
        a_ref[0] = a_ref[1]
        f = _dot(a_ref[0], wdn_ref[...])
        o_ref[...] = xp_ref[...] + _rms(f, gain_ref[3:4, :])

    def project(h, c):
        for r0 in range(0, tile, UP_ROWS):
            for col in (c * FF_CHUNK, D_FF + c * FF_CHUNK):
                u = _dot(h[r0:r0 + UP_ROWS], wup_ref[:, col:col + FF_CHUNK])
                for k in range(FF_CHUNK // LANES):
                    slab = u_ref.at[col // LANES + k]
                    slab[CONV_HALO + r0:CONV_HALO + r0 + UP_ROWS, :] = u[:, k * LANES:(k + 1) * LANES]

    def conv(col):
        parts = []
        for k in range(FF_CHUNK // LANES):
            cols = slice(col + k * LANES, col + (k + 1) * LANES)
            slab = u_ref.at[cols.start // LANES]
            taps = [slab[CONV_HALO - 2 + j:CONV_HALO - 2 + j + tile, :] * cw_ref[j:j + 1, cols]
                    for j in range(3)]
            parts.append(cb_ref[:, cols] + taps[0] + taps[1] + taps[2])
            slab[0:CONV_HALO, :] = slab[tile:tile + CONV_HALO, :]
        return jnp.concatenate(parts, axis=-1)

    def gated(c):
        gate = conv(c * FF_CHUNK)
        val = conv(D_FF + c * FF_CHUNK)
        a_ref[1, :, c * FF_CHUNK:(c + 1) * FF_CHUNK] = (_silu(gate) * val).astype(_BF16)

    def step(with_up, with_gate, with_down):
        if with_down:
            down()
        h = _rms(xc_ref[...], gain_ref[2:3, :]).astype(_BF16) if with_up else None
        lead = GATE_LEAD if with_up else n_chunks
        if with_gate:
            for c in range(lead):
                gated(c)
        if with_up:
            for c in range(n_chunks):
                project(h, c)
                if with_gate and c + lead < n_chunks:
                    gated(c + lead)

    pl.when(i == 0)(lambda: step(True, False, False))
    pl.when(i == 1)(lambda: step(True, True, False))
    pl.when(jnp.logical_and(i >= 2, i < n_tiles))(lambda: step(True, True, True))
    pl.when(i == n_tiles)(lambda: step(False, True, True))
    pl.when(i == n_tiles + 1)(lambda: step(False, False, True))


def _mlp_layer(x, gains, layer, w_up, conv_w, conv_b, w_down):
    b, s, d = x.shape
    tile = SEQ_TILE
    n_tiles = b * s // tile
    const = lambda *_: (0, 0)
    resident = lambda a: pl.BlockSpec((None,) + a.shape[1:], lambda *_: (layer, 0, 0),
                                      pipeline_mode=pl.Buffered(1))
    cur = lambda i: (jnp.minimum(i, n_tiles - 1), 0)
    done = lambda i: (jnp.maximum(i - 2, 0), 0)
    x2 = x.reshape(b * s, d)
    out = pl.pallas_call(
        functools.partial(_mlp_kernel, tile=tile, tiles_per_row=s // tile, n_tiles=n_tiles),
        grid=(n_tiles + 2,),
        in_specs=[
            pl.BlockSpec((tile, d), cur),
            pl.BlockSpec((tile, d), done),
            pl.BlockSpec(gains.shape, const),
            resident(w_up),
            pl.BlockSpec(conv_w.shape, const),
            pl.BlockSpec(conv_b.shape, const),
            resident(w_down),
        ],
        out_specs=pl.BlockSpec((tile, d), done),
        out_shape=jax.ShapeDtypeStruct(x2.shape, x.dtype),
        scratch_shapes=[
            pltpu.VMEM((2 * D_FF // LANES, CONV_HALO + tile, LANES), _F32),
            pltpu.VMEM((2, tile, D_FF), _BF16),
        ],
        compiler_params=pltpu.CompilerParams(
            dimension_semantics=("arbitrary",), vmem_limit_bytes=VMEM_LIMIT_BYTES),
        name="conv_glu_mlp",
    )(x2, x2, gains, w_up, conv_w, conv_b, w_down)
    return out.reshape(b, s, d)


def _ret_kernel(xc_ref, pos_ref, xd_ref, gain_ref, freq_ref, win_ref, intra_ref, cross_ref,
                indec_ref, cdec_ref, gn_ref, wout_ref, o_ref, state_ref, qkh_ref, y_ref,
                *, tile, tiles_per_row, n_tiles):
    i = pl.program_id(0)
    n_sub = tile // RET_CHUNK
    q0, k0, kd0, h0 = 0, RET_QK_TOTAL, 2 * RET_QK_TOTAL, 3 * RET_QK_TOTAL

    @pl.when((i - 1) % tiles_per_row == 0)
    def _():
        state_ref[...] = jnp.zeros(state_ref.shape, _F32)

    def output():
        y_ref[0] = y_ref[1]
        f = _dot(y_ref[0], wout_ref[...])
        o_ref[...] = xd_ref[...] + _rms(f, gain_ref[1:2, :])

    def recurrence():
        qkh_ref[0] = qkh_ref[1]
        hp = qkh_ref[0, :, h0:h0 + D_MODEL]
        for hd in range(RET_HEADS):
            vc = slice(2 * RET_QK_TOTAL + hd * RET_V_DIM, 2 * RET_QK_TOTAL + (hd + 1) * RET_V_DIM)
            gc = slice(2 * RET_QK_TOTAL + RET_V_TOTAL + hd * RET_V_DIM,
                       2 * RET_QK_TOTAL + RET_V_TOTAL + (hd + 1) * RET_V_DIM)
            yc = slice(hd * RET_V_DIM, (hd + 1) * RET_V_DIM)
            hc = slice(hd * RET_QK_DIM, (hd + 1) * RET_QK_DIM)
            v = _dot(hp, win_ref[:, vc]).astype(_BF16)
            gate = _dot(hp, win_ref[:, gc])
            intra = intra_ref[hd]
            cross = cross_ref[hd][:, 0:1]
            cdec = cdec_ref[hd][0:1, 0:1]
            for sub in range(n_sub):
                rows = slice(sub * RET_CHUNK, (sub + 1) * RET_CHUNK)
                qs = qkh_ref[0, rows, q0 + hc.start:q0 + hc.stop]
                ks = qkh_ref[0, rows, k0 + hc.start:k0 + hc.stop]
                kd = qkh_ref[0, rows, kd0 + hc.start:kd0 + hc.stop]
                vs = v[rows]
                state = state_ref[hd]
                scores = lax.dot_general(qs, ks, (((1,), (1,)), ((), ())),
                                         preferred_element_type=_F32) * intra
                o = _dot(scores.astype(_BF16), vs) + _dot(qs, state.astype(_BF16)) * cross
                state_ref[hd] = state * cdec + lax.dot_general(
                    kd, vs, (((0,), (0,)), ((), ())), preferred_element_type=_F32)
                mu = jnp.mean(o, axis=-1, keepdims=True)
                dev = o - mu
                var = jnp.mean(dev * dev, axis=-1, keepdims=True)
                on = dev * lax.rsqrt(var + EPS) * gn_ref[:, yc]
                y_ref[1, rows, yc] = (_silu(gate[rows]) * on).astype(_BF16)

    def projection():
        h = _rms(xc_ref[...], gain_ref[0:1, :]).astype(_BF16)
        qkh_ref[1, :, h0:h0 + D_MODEL] = h
        ang = pos_ref[...].astype(_F32) * freq_ref[...]
        cos, sin = jnp.cos(ang), jnp.sin(ang)
        q_scale = RET_QK_DIM ** -0.5
        half = RET_QK_DIM // 2

        def rotary(a, c, sn):
            a1, a2 = a[:, :half], a[:, half:]
            return jnp.concatenate([a1 * c - a2 * sn, a2 * c + a1 * sn], axis=-1)

        for hd in range(RET_HEADS):
            hc = slice(hd * RET_QK_DIM, (hd + 1) * RET_QK_DIM)
            q = rotary(_dot(h, win_ref[:, q0 + hc.start:q0 + hc.stop]), cos * q_scale, sin * q_scale)
            k = rotary(_dot(h, win_ref[:, k0 + hc.start:k0 + hc.stop]), cos, sin)
            qkh_ref[1, :, q0 + hc.start:q0 + hc.stop] = q.astype(_BF16)
            qkh_ref[1, :, k0 + hc.start:k0 + hc.stop] = k.astype(_BF16)
            indec = indec_ref[hd][:, 0:1]
            for sub in range(n_sub):
                rows = slice(sub * RET_CHUNK, (sub + 1) * RET_CHUNK)
                qkh_ref[1, rows, kd0 + hc.start:kd0 + hc.stop] = (k[rows] * indec).astype(_BF16)

    def step(with_proj, with_rec, with_out):
        if with_out:
            output()
        if with_rec:
            recurrence()
        if with_proj:
            projection()

    pl.when(i == 0)(lambda: step(True, False, False))
    pl.when(i == 1)(lambda: step(True, True, False))
    pl.when(jnp.logical_and(i >= 2, i < n_tiles))(lambda: step(True, True, True))
    pl.when(i == n_tiles)(lambda: step(False, True, True))
    pl.when(i == n_tiles + 1)(lambda: step(False, False, True))


def _retention_tables():
    log_gamma = jnp.log(1.0 - 2.0 ** (-5.0 - jnp.arange(RET_HEADS, dtype=_F32)))
    i = jnp.arange(RET_CHUNK, dtype=_F32)
    rel = i[:, None] - i[None, :]
    intra = jnp.where(rel >= 0, jnp.exp(jnp.maximum(rel, 0.0) * log_gamma[:, None, None]), 0.0)
    cross = jnp.exp((i + 1.0) * log_gamma[:, None])
    indec = jnp.exp((RET_CHUNK - 1.0 - i) * log_gamma[:, None])
    cdec = jnp.exp(RET_CHUNK * log_gamma)
    lanes = 128
    rep = lambda a: jnp.broadcast_to(a[:, :, None], a.shape + (lanes,))
    cdec = jnp.broadcast_to(cdec[:, None, None], (RET_HEADS, 8, lanes))
    inv_freq = ROPE_BASE ** (-jnp.arange(0, RET_QK_DIM, 2, dtype=_F32) / RET_QK_DIM)
    return inv_freq[None, :], intra, rep(cross), rep(indec), cdec


def _retention_layer(x, positions, gains, w_in, gn_gain, w_out):
    b, s, d = x.shape
    tile = SEQ_TILE
    n_tiles = b * s // tile
    inv_freq, intra, cross, indec, cdec = _retention_tables()
    const2 = lambda *_: (0, 0)
    const3 = lambda *_: (0, 0, 0)
    once2 = functools.partial(pl.BlockSpec, index_map=const2, pipeline_mode=pl.Buffered(1))
    once3 = functools.partial(pl.BlockSpec, index_map=const3, pipeline_mode=pl.Buffered(1))
    cur = lambda i: (jnp.minimum(i, n_tiles - 1), 0)
    done = lambda i: (jnp.maximum(i - 2, 0), 0)
    x2 = x.reshape(b * s, d)
    out = pl.pallas_call(
        functools.partial(_ret_kernel, tile=tile, tiles_per_row=s // tile, n_tiles=n_tiles),
        grid=(n_tiles + 2,),
        in_specs=[
            pl.BlockSpec((tile, d), cur),
            pl.BlockSpec((tile, 1), cur),
            pl.BlockSpec((tile, d), done),
            pl.BlockSpec(gains.shape, const2),
            pl.BlockSpec(inv_freq.shape, const2),
            once2(w_in.shape),
            once3(intra.shape),
            once3(cross.shape),
            once3(indec.shape),
            once3(cdec.shape),
            pl.BlockSpec(gn_gain.shape, const2),
            once2(w_out.shape),
        ],
        out_specs=pl.BlockSpec((tile, d), done),
        out_shape=jax.ShapeDtypeStruct(x2.shape, x.dtype),
        scratch_shapes=[
            pltpu.VMEM((RET_HEADS, RET_QK_DIM, RET_V_DIM), _F32),
            pltpu.VMEM((2, tile, 3 * RET_QK_TOTAL + D_MODEL), _BF16),
            pltpu.VMEM((2, tile, RET_V_TOTAL), _BF16),
        ],
        compiler_params=pltpu.CompilerParams(
            dimension_semantics=("arbitrary",), vmem_limit_bytes=VMEM_LIMIT_BYTES),
        name="retention",
    )(x2, positions.reshape(b * s, 1), x2, gains, inv_freq, w_in, intra, cross, indec, cdec,
      gn_gain, w_out)
    return out.reshape(b, s, d)


def kernel(x, positions, norm_gain, pool_w, pool_scale, ret_w_in, ret_gn_gain, ret_w_out,
           mlp_w_up, mlp_conv_w, mlp_conv_b, mlp_w_down):
    bf = lambda a: a.astype(_BF16)
    w_up, w_down = bf(mlp_w_up), bf(mlp_w_down)
    x = _pool_layer(x, norm_gain[0], bf(pool_w[0]), pool_scale[0][None, :])
    x = _mlp_layer(x, norm_gain[0], 0, w_up, mlp_conv_w[0], mlp_conv_b[0][None, :], w_down)
    x = _retention_layer(x, positions, norm_gain[1], bf(ret_w_in[0]), ret_gn_gain[0][None, :],
                         bf(ret_w_out[0]))
    x = _mlp_layer(x, norm_gain[1], 1, w_up, mlp_conv_w[1], mlp_conv_b[1][None, :], w_down)
    return x
```

```python
import functools

import jax
import jax.numpy as jnp
from jax import lax
from jax.experimental import pallas as pl
from jax.experimental.pallas import tpu as pltpu

LANES = 128
D_MODEL = 1024
POOL_WINDOWS = (2, 4, 8, 16)
POOL_GROUP_DIM = D_MODEL // len(POOL_WINDOWS)
POOL_HALO = 16
RET_HEADS = 4
RET_QK_DIM = 256
RET_V_DIM = 512
RET_QK_TOTAL = RET_HEADS * RET_QK_DIM
RET_V_TOTAL = RET_HEADS * RET_V_DIM
RET_CHUNK = 256
ROPE_BASE = 10000.0
D_FF = 2816
FF_CHUNK = 256
UP_ROWS = 128
GATE_LEAD = 6
CONV_HALO = 8
EPS = 1e-6

SEQ_TILE = 512
VMEM_LIMIT_BYTES = 56 * 1024 * 1024

_F32 = jnp.float32
_BF16 = jnp.bfloat16


def _dot(a, b):
    return jnp.dot(a, b, preferred_element_type=_F32)


def _rms(x, gain):
    var = jnp.mean(x * x, axis=-1, keepdims=True)
    return x * lax.rsqrt(var + EPS) * gain


def _silu(x):
    return x * (1.0 / (1.0 + jnp.exp(-x)))


def _pipeline_steps(i, n_tiles, step):
    pl.when(i == 0)(lambda: step(0, True, False, False))
    pl.when(i == 1)(lambda: step(1, True, True, False))
    for parity in (0, 1):
        steady = jnp.logical_and(jnp.logical_and(i >= 2, i < n_tiles), i % 2 == parity)
        pl.when(steady)(functools.partial(step, parity, True, True, True))
    pl.when(i == n_tiles)(lambda: step(n_tiles % 2, False, True, True))
    pl.when(i == n_tiles + 1)(lambda: step((n_tiles + 1) % 2, False, False, True))


def _pool_kernel(x_ref, gain_ref, w_ref, scale_ref, o_ref, ext_ref, *, tile):
    s = pl.program_id(1)
    x = x_ref[...]
    h = _rms(x, gain_ref[0:1, :])

    @pl.when(s == 0)
    def _():
        ext_ref[:, 0:POOL_HALO, :] = jnp.zeros((D_MODEL // LANES, POOL_HALO, LANES), _F32)

    t = s * tile + lax.broadcasted_iota(jnp.int32, (tile, 1), 0)
    parts = []
    for g, win in enumerate(POOL_WINDOWS):
        inv_count = 1.0 / jnp.minimum(t + 1, win).astype(_F32)
        diffs = []
        for k in range(POOL_GROUP_DIM // LANES):
            lo = g * POOL_GROUP_DIM + k * LANES
            slab = ext_ref.at[lo // LANES]
            hk = h[:, lo:lo + LANES]
            slab[POOL_HALO:POOL_HALO + tile, :] = hk
            acc = hk
            for j in range(1, win):
                acc = acc + slab[POOL_HALO - j:POOL_HALO - j + tile, :]
            slab[0:POOL_HALO, :] = hk[tile - POOL_HALO:, :]
            diffs.append(acc * inv_count - hk)
        diff = jnp.concatenate(diffs, axis=-1)
        parts.append(_dot(diff.astype(_BF16), w_ref[g]))
    y = jnp.concatenate(parts, axis=-1) * scale_ref[...]
    o_ref[...] = x + _rms(y, gain_ref[1:2, :])


def _pool_layer(x, gains, w, scale):
    b, s, d = x.shape
    tile = SEQ_TILE
    const = lambda *_: (0, 0)
    return pl.pallas_call(
        functools.partial(_pool_kernel, tile=tile),
        grid=(b, s // tile),
        in_specs=[
            pl.BlockSpec((None, tile, d), lambda i, j: (i, j, 0)),
            pl.BlockSpec(gains.shape, const),
            pl.BlockSpec(w.shape, lambda *_: (0, 0, 0)),
            pl.BlockSpec(scale.shape, const),
        ],
        out_specs=pl.BlockSpec((None, tile, d), lambda i, j: (i, j, 0)),
        out_shape=jax.ShapeDtypeStruct(x.shape, x.dtype),
        scratch_shapes=[pltpu.VMEM((d // LANES, POOL_HALO + tile, LANES), _F32)],
        compiler_params=pltpu.CompilerParams(
            dimension_semantics=("arbitrary", "arbitrary"), vmem_limit_bytes=VMEM_LIMIT_BYTES),
        name="pool_mixer",
    )(x, gains, w, scale)


def _mlp_kernel(xc_ref, xp_ref, gain_ref, wup_ref, cw_ref, cb_ref, wdn_ref, o_ref, u_ref, a_ref,
                *, tile, tiles_per_row, n_tiles):
    i = pl.program_id(0)
    n_chunks = D_FF // FF_CHUNK

    @pl.when((i - 1) % tiles_per_row == 0)
    def _():
        u_ref[:, 0:CONV_HALO, :] = jnp.zeros((2 * D_FF // LANES, CONV_HALO, LANES), _F32)

    def down(slot):
        f = _dot(a_ref[slot], wdn_ref[...])
        o_ref[...] = xp_ref[...] + _rms(f, gain_ref[3:4, :])

    def project(h, c):
        for r0 in range(0, tile, UP_ROWS):
            for col in (c * FF_CHUNK, D_FF + c * FF_CHUNK):
                u = _dot(h[r0:r0 + UP_ROWS], wup_ref[:, col:col + FF_CHUNK])
                for k in range(FF_CHUNK // LANES):
                    slab = u_ref.at[col // LANES + k]
                    slab[CONV_HALO + r0:CONV_HALO + r0 + UP_ROWS, :] = u[:, k * LANES:(k + 1) * LANES]

    def conv(col):
        parts = []
        for k in range(FF_CHUNK // LANES):
            cols = slice(col + k * LANES, col + (k + 1) * LANES)
            slab = u_ref.at[cols.start // LANES]
            taps = [slab[CONV_HALO - 2 + j:CONV_HALO - 2 + j + tile, :] * cw_ref[j:j + 1, cols]
                    for j in range(3)]
            parts.append(cb_ref[:, cols] + taps[0] + taps[1] + taps[2])
            slab[0:CONV_HALO, :] = slab[tile:tile + CONV_HALO, :]
        return jnp.concatenate(parts, axis=-1)

    def gated(slot, c):
        gate = conv(c * FF_CHUNK)
        val = conv(D_FF + c * FF_CHUNK)
        a_ref[slot, :, c * FF_CHUNK:(c + 1) * FF_CHUNK] = (_silu(gate) * val).astype(_BF16)

    def step(parity, with_up, with_gate, with_down):
        if with_down:
            down(1 - parity)
        h = _rms(xc_ref[...], gain_ref[2:3, :]).astype(_BF16) if with_up else None
        lead = GATE_LEAD if with_up else n_chunks
        if with_gate:
            for c in range(lead):
                gated(parity, c)
        if with_up:
            for c in range(n_chunks):
                project(h, c)
                if with_gate and c + lead < n_chunks:
                    gated(parity, c + lead)

    _pipeline_steps(i, n_tiles, step)


def _mlp_layer(x, gains, layer, w_up, conv_w, conv_b, w_down):
    b, s, d = x.shape
    tile = SEQ_TILE
    n_tiles = b * s // tile
    const = lambda *_: (0, 0)
    resident = lambda a: pl.BlockSpec((None,) + a.shape[1:], lambda *_: (layer, 0, 0),
                                      pipeline_mode=pl.Buffered(1))
    cur = lambda i: (jnp.minimum(i, n_tiles - 1), 0)
    done = lambda i: (jnp.maximum(i - 2, 0), 0)
    x2 = x.reshape(b * s, d)
    out = pl.pallas_call(
        functools.partial(_mlp_kernel, tile=tile, tiles_per_row=s // tile, n_tiles=n_tiles),
        grid=(n_tiles + 2,),
        in_specs=[
            pl.BlockSpec((tile, d), cur),
            pl.BlockSpec((tile, d), done),
            pl.BlockSpec(gains.shape, const),
            resident(w_up),
            pl.BlockSpec(conv_w.shape, const),
            pl.BlockSpec(conv_b.shape, const),
            resident(w_down),
        ],
        out_specs=pl.BlockSpec((tile, d), done),
        out_shape=jax.ShapeDtypeStruct(x2.shape, x.dtype),
        scratch_shapes=[
            pltpu.VMEM((2 * D_FF // LANES, CONV_HALO + tile, LANES), _F32),
            pltpu.VMEM((2, tile, D_FF), _BF16),
        ],
        compiler_params=pltpu.CompilerParams(
            dimension_semantics=("arbitrary",), vmem_limit_bytes=VMEM_LIMIT_BYTES),
        name="conv_glu_mlp",
    )(x2, x2, gains, w_up, conv_w, conv_b, w_down)
    return out.reshape(b, s, d)


def _ret_kernel(xc_ref, pos_ref, xd_ref, gain_ref, freq_ref, win_ref, intra_ref, cross_ref,
                indec_ref, cdec_ref, gn_ref, wout_ref, o_ref, state_ref, qkh_ref, y_ref,
                *, tile, tiles_per_row, n_tiles):
    i = pl.program_id(0)
    n_sub = tile // RET_CHUNK
    q0, k0, kd0, h0 = 0, RET_QK_TOTAL, 2 * RET_QK_TOTAL, 3 * RET_QK_TOTAL

    @pl.when((i - 1) % tiles_per_row == 0)
    def _():
        state_ref[...] = jnp.zeros(state_ref.shape, _F32)

    def output(src):
        f = _dot(y_ref[src], wout_ref[...])
        o_ref[...] = xd_ref[...] + _rms(f, gain_ref[1:2, :])

    def recurrence(src, dst):
        for hd in range(RET_HEADS):
            vc = slice(2 * RET_QK_TOTAL + hd * RET_V_DIM, 2 * RET_QK_TOTAL + (hd + 1) * RET_V_DIM)
            gc = slice(2 * RET_QK_TOTAL + RET_V_TOTAL + hd * RET_V_DIM,
                       2 * RET_QK_TOTAL + RET_V_TOTAL + (hd + 1) * RET_V_DIM)
            yc = slice(hd * RET_V_DIM, (hd + 1) * RET_V_DIM)
            hc = slice(hd * RET_QK_DIM, (hd + 1) * RET_QK_DIM)
            intra = intra_ref[hd]
            cross = cross_ref[hd][:, 0:1]
            cdec = cdec_ref[hd][0:1, 0:1]
            for sub in range(n_sub):
                rows = slice(sub * RET_CHUNK, (sub + 1) * RET_CHUNK)
                hp = qkh_ref[src, rows, h0:h0 + D_MODEL]
                qs = qkh_ref[src, rows, q0 + hc.start:q0 + hc.stop]
                ks = qkh_ref[src, rows, k0 + hc.start:k0 + hc.stop]
                kd = qkh_ref[src, rows, kd0 + hc.start:kd0 + hc.stop]
                vs = _dot(hp, win_ref[:, vc]).astype(_BF16)
                state = state_ref[hd]
                scores = lax.dot_general(qs, ks, (((1,), (1,)), ((), ())),
                                         preferred_element_type=_F32) * intra
                o = _dot(scores.astype(_BF16), vs) + _dot(qs, state.astype(_BF16)) * cross
                state_ref[hd] = state * cdec + lax.dot_general(
                    kd, vs, (((0,), (0,)), ((), ())), preferred_element_type=_F32)
                mu = jnp.mean(o, axis=-1, keepdims=True)
                dev = o - mu
                var = jnp.mean(dev * dev, axis=-1, keepdims=True)
                on = dev * lax.rsqrt(var + EPS) * gn_ref[:, yc]
                gate = _dot(hp, win_ref[:, gc])
                y_ref[dst, rows, yc] = (_silu(gate) * on).astype(_BF16)

    def projection(dst):
        h = _rms(xc_ref[...], gain_ref[0:1, :]).astype(_BF16)
        qkh_ref[dst, :, h0:h0 + D_MODEL] = h
        ang = pos_ref[...].astype(_F32) * freq_ref[...]
        cos, sin = jnp.cos(ang), jnp.sin(ang)
        q_scale = RET_QK_DIM ** -0.5
        half = RET_QK_DIM // 2

        def rotary(a, c, sn):
            a1, a2 = a[:, :half], a[:, half:]
            return jnp.concatenate([a1 * c - a2 * sn, a2 * c + a1 * sn], axis=-1)

        for hd in range(RET_HEADS):
            hc = slice(hd * RET_QK_DIM, (hd + 1) * RET_QK_DIM)
            q = rotary(_dot(h, win_ref[:, q0 + hc.start:q0 + hc.stop]), cos * q_scale, sin * q_scale)
            k = rotary(_dot(h, win_ref[:, k0 + hc.start:k0 + hc.stop]), cos, sin)
            qkh_ref[dst, :, q0 + hc.start:q0 + hc.stop] = q.astype(_BF16)
            qkh_ref[dst, :, k0 + hc.start:k0 + hc.stop] = k.astype(_BF16)
            indec = indec_ref[hd][:, 0:1]
            for sub in range(n_sub):
                rows = slice(sub * RET_CHUNK, (sub + 1) * RET_CHUNK)
                qkh_ref[dst, rows, kd0 + hc.start:kd0 + hc.stop] = (k[rows] * indec).astype(_BF16)

    def step(parity, with_proj, with_rec, with_out):
        if with_out:
            output(1 - parity)
        if with_rec:
            recurrence(1 - parity, parity)
        if with_proj:
            projection(parity)

    _pipeline_steps(i, n_tiles, step)


def _retention_tables():
    log_gamma = jnp.log(1.0 - 2.0 ** (-5.0 - jnp.arange(RET_HEADS, dtype=_F32)))
    i = jnp.arange(RET_CHUNK, dtype=_F32)
    rel = i[:, None] - i[None, :]
    intra = jnp.where(rel >= 0, jnp.exp(jnp.maximum(rel, 0.0) * log_gamma[:, None, None]), 0.0)
    cross = jnp.exp((i + 1.0) * log_gamma[:, None])
    indec = jnp.exp((RET_CHUNK - 1.0 - i) * log_gamma[:, None])
    cdec = jnp.exp(RET_CHUNK * log_gamma)
    lanes = 128
    rep = lambda a: jnp.broadcast_to(a[:, :, None], a.shape + (lanes,))
    cdec = jnp.broadcast_to(cdec[:, None, None], (RET_HEADS, 8, lanes))
    inv_freq = ROPE_BASE ** (-jnp.arange(0, RET_QK_DIM, 2, dtype=_F32) / RET_QK_DIM)
    return inv_freq[None, :], intra, rep(cross), rep(indec), cdec


def _retention_layer(x, positions, gains, w_in, gn_gain, w_out):
    b, s, d = x.shape
    tile = SEQ_TILE
    n_tiles = b * s // tile
    inv_freq, intra, cross, indec, cdec = _retention_tables()
    const2 = lambda *_: (0, 0)
    const3 = lambda *_: (0, 0, 0)
    once2 = functools.partial(pl.BlockSpec, index_map=const2, pipeline_mode=pl.Buffered(1))
    once3 = functools.partial(pl.BlockSpec, index_map=const3, pipeline_mode=pl.Buffered(1))
    cur = lambda i: (jnp.minimum(i, n_tiles - 1), 0)
    done = lambda i: (jnp.maximum(i - 2, 0), 0)
    x2 = x.reshape(b * s, d)
    out = pl.pallas_call(
        functools.partial(_ret_kernel, tile=tile, tiles_per_row=s // tile, n_tiles=n_tiles),
        grid=(n_tiles + 2,),
        in_specs=[
            pl.BlockSpec((tile, d), cur),
            pl.BlockSpec((tile, 1), cur),
            pl.BlockSpec((tile, d), done),
            pl.BlockSpec(gains.shape, const2),
            pl.BlockSpec(inv_freq.shape, const2),
            once2(w_in.shape),
            once3(intra.shape),
            once3(cross.shape),
            once3(indec.shape),
            once3(cdec.shape),
            pl.BlockSpec(gn_gain.shape, const2),
            once2(w_out.shape),
        ],
        out_specs=pl.BlockSpec((tile, d), done),
        out_shape=jax.ShapeDtypeStruct(x2.shape, x.dtype),
        scratch_shapes=[
            pltpu.VMEM((RET_HEADS, RET_QK_DIM, RET_V_DIM), _F32),
            pltpu.VMEM((2, tile, 3 * RET_QK_TOTAL + D_MODEL), _BF16),
            pltpu.VMEM((2, tile, RET_V_TOTAL), _BF16),
        ],
        compiler_params=pltpu.CompilerParams(
            dimension_semantics=("arbitrary",), vmem_limit_bytes=VMEM_LIMIT_BYTES),
        name="retention",
    )(x2, positions.reshape(b * s, 1), x2, gains, inv_freq, w_in, intra, cross, indec, cdec,
      gn_gain, w_out)
    return out.reshape(b, s, d)


def kernel(x, positions, norm_gain, pool_w, pool_scale, ret_w_in, ret_gn_gain, ret_w_out,
           mlp_w_up, mlp_conv_w, mlp_conv_b, mlp_w_down):
    bf = lambda a: a.astype(_BF16)
    w_up, w_down = bf(mlp_w_up), bf(mlp_w_down)
    x = _pool_layer(x, norm_gain[0], bf(pool_w[0]), pool_scale[0][None, :])
    x = _mlp_layer(x, norm_gain[0], 0, w_up, mlp_conv_w[0], mlp_conv_b[0][None, :], w_down)
    x = _retention_layer(x, positions, norm_gain[1], bf(ret_w_in[0]), ret_gn_gain[0][None, :],
                         bf(ret_w_out[0]))
    x = _mlp_layer(x, norm_gain[1], 1, w_up, mlp_conv_w[1], mlp_conv_b[1][None, :], w_down)
    return x
```

```python
import functools

import jax
import jax.numpy as jnp
from jax import lax
from jax.experimental import pallas as pl
from jax.experimental.pallas import tpu as pltpu

LANES = 128
D_MODEL = 1024
POOL_WINDOWS = (2, 4, 8, 16)
POOL_GROUP_DIM = D_MODEL // len(POOL_WINDOWS)
POOL_HALO = 16
RET_HEADS = 4
RET_QK_DIM = 256
RET_V_DIM = 512
RET_QK_TOTAL = RET_HEADS * RET_QK_DIM
RET_V_TOTAL = RET_HEADS * RET_V_DIM
RET_CHUNK = 256
ROPE_BASE = 10000.0
D_FF = 2816
FF_CHUNK = 256
UP_ROWS = 128
GATE_LEAD = 6
CONV_HALO = 8
EPS = 1e-6

SEQ_TILE = 512
VMEM_LIMIT_BYTES = 56 * 1024 * 1024

_F32 = jnp.float32
_BF16 = jnp.bfloat16


def _dot(a, b):
    return jnp.dot(a, b, preferred_element_type=_F32)


def _rms(x, gain):
    var = jnp.mean(x * x, axis=-1, keepdims=True)
    return x * lax.rsqrt(var + EPS) * gain


def _silu(x):
    return x * (1.0 / (1.0 + jnp.exp(-x)))


def _pipeline_steps(i, n_tiles, step, init_last_stage_input):
    @pl.when(i == 0)
    def _():
        init_last_stage_input()
        step(0, True, False, False)

    for parity in (0, 1):
        steady = jnp.logical_and(jnp.logical_and(i >= 1, i <= n_tiles), i % 2 == parity)
        pl.when(steady)(functools.partial(step, parity, True, True, True))
    pl.when(i == n_tiles + 1)(lambda: step((n_tiles + 1) % 2, False, False, True))


def _pool_kernel(x_ref, gain_ref, w_ref, scale_ref, o_ref, ext_ref, *, tile):
    s = pl.program_id(1)
    x = x_ref[...]
    h = _rms(x, gain_ref[0:1, :])

    @pl.when(s == 0)
    def _():
        ext_ref[:, 0:POOL_HALO, :] = jnp.zeros((D_MODEL // LANES, POOL_HALO, LANES), _F32)

    t = s * tile + lax.broadcasted_iota(jnp.int32, (tile, 1), 0)
    parts = []
    for g, win in enumerate(POOL_WINDOWS):
        inv_count = 1.0 / jnp.minimum(t + 1, win).astype(_F32)
        diffs = []
        for k in range(POOL_GROUP_DIM // LANES):
            lo = g * POOL_GROUP_DIM + k * LANES
            slab = ext_ref.at[lo // LANES]
            hk = h[:, lo:lo + LANES]
            slab[POOL_HALO:POOL_HALO + tile, :] = hk
            acc = hk
            for j in range(1, win):
                acc = acc + slab[POOL_HALO - j:POOL_HALO - j + tile, :]
            slab[0:POOL_HALO, :] = hk[tile - POOL_HALO:, :]
            diffs.append(acc * inv_count - hk)
        diff = jnp.concatenate(diffs, axis=-1)
        parts.append(_dot(diff.astype(_BF16), w_ref[g]))
    y = jnp.concatenate(parts, axis=-1) * scale_ref[...]
    o_ref[...] = x + _rms(y, gain_ref[1:2, :])


def _pool_layer(x, gains, w, scale):
    b, s, d = x.shape
    tile = SEQ_TILE
    const = lambda *_: (0, 0)
    return pl.pallas_call(
        functools.partial(_pool_kernel, tile=tile),
        grid=(b, s // tile),
        in_specs=[
            pl.BlockSpec((None, tile, d), lambda i, j: (i, j, 0)),
            pl.BlockSpec(gains.shape, const),
            pl.BlockSpec(w.shape, lambda *_: (0, 0, 0)),
            pl.BlockSpec(scale.shape, const),
        ],
        out_specs=pl.BlockSpec((None, tile, d), lambda i, j: (i, j, 0)),
        out_shape=jax.ShapeDtypeStruct(x.shape, x.dtype),
        scratch_shapes=[pltpu.VMEM((d // LANES, POOL_HALO + tile, LANES), _F32)],
        compiler_params=pltpu.CompilerParams(
            dimension_semantics=("arbitrary", "arbitrary"), vmem_limit_bytes=VMEM_LIMIT_BYTES),
        name="pool_mixer",
    )(x, gains, w, scale)


def _mlp_kernel(xc_ref, xp_ref, gain_ref, wup_ref, cw_ref, cb_ref, wdn_ref, o_ref, u_ref, a_ref,
                *, tile, tiles_per_row, n_tiles):
    i = pl.program_id(0)
    n_chunks = D_FF // FF_CHUNK

    @pl.when((i - 1) % tiles_per_row == 0)
    def _():
        u_ref[:, 0:CONV_HALO, :] = jnp.zeros((2 * D_FF // LANES, CONV_HALO, LANES), _F32)

    def down(slot):
        f = _dot(a_ref[slot], wdn_ref[...])
        o_ref[...] = xp_ref[...] + _rms(f, gain_ref[3:4, :])

    def project(h, c):
        for r0 in range(0, tile, UP_ROWS):
            for col in (c * FF_CHUNK, D_FF + c * FF_CHUNK):
                u = _dot(h[r0:r0 + UP_ROWS], wup_ref[:, col:col + FF_CHUNK])
                for k in range(FF_CHUNK // LANES):
                    slab = u_ref.at[col // LANES + k]
                    slab[CONV_HALO + r0:CONV_HALO + r0 + UP_ROWS, :] = u[:, k * LANES:(k + 1) * LANES]

    def conv(col):
        parts = []
        for k in range(FF_CHUNK // LANES):
            cols = slice(col + k * LANES, col + (k + 1) * LANES)
            slab = u_ref.at[cols.start // LANES]
            taps = [slab[CONV_HALO - 2 + j:CONV_HALO - 2 + j + tile, :] * cw_ref[j:j + 1, cols]
                    for j in range(3)]
            parts.append(cb_ref[:, cols] + taps[0] + taps[1] + taps[2])
            slab[0:CONV_HALO, :] = slab[tile:tile + CONV_HALO, :]
        return jnp.concatenate(parts, axis=-1)

    def gated(slot, c):
        gate = conv(c * FF_CHUNK)
        val = conv(D_FF + c * FF_CHUNK)
        a_ref[slot, :, c * FF_CHUNK:(c + 1) * FF_CHUNK] = (_silu(gate) * val).astype(_BF16)

    def step(parity, with_up, with_gate, with_down):
        if with_down:
            down(1 - parity)
        h = _rms(xc_ref[...], gain_ref[2:3, :]).astype(_BF16) if with_up else None
        lead = GATE_LEAD if with_up else n_chunks
        if with_gate:
            for c in range(lead):
                gated(parity, c)
        if with_up:
            for c in range(n_chunks):
                project(h, c)
                if with_gate and c + lead < n_chunks:
                    gated(parity, c + lead)

    def zero_first_down_input():
        a_ref[0] = jnp.zeros(a_ref.shape[1:], _BF16)

    _pipeline_steps(i, n_tiles, step, zero_first_down_input)


def _mlp_layer(x, gains, layer, w_up, conv_w, conv_b, w_down):
    b, s, d = x.shape
    tile = SEQ_TILE
    n_tiles = b * s // tile
    const = lambda *_: (0, 0)
    resident = lambda a: pl.BlockSpec((None,) + a.shape[1:], lambda *_: (layer, 0, 0),
                                      pipeline_mode=pl.Buffered(1))
    cur = lambda i: (jnp.minimum(i, n_tiles - 1), 0)
    done = lambda i: (jnp.maximum(i - 2, 0), 0)
    x2 = x.reshape(b * s, d)
    out = pl.pallas_call(
        functools.partial(_mlp_kernel, tile=tile, tiles_per_row=s // tile, n_tiles=n_tiles),
        grid=(n_tiles + 2,),
        in_specs=[
            pl.BlockSpec((tile, d), cur),
            pl.BlockSpec((tile, d), done),
            pl.BlockSpec(gains.shape, const),
            resident(w_up),
            pl.BlockSpec(conv_w.shape, const),
            pl.BlockSpec(conv_b.shape, const),
            resident(w_down),
        ],
        out_specs=pl.BlockSpec((tile, d), done),
        out_shape=jax.ShapeDtypeStruct(x2.shape, x.dtype),
        scratch_shapes=[
            pltpu.VMEM((2 * D_FF // LANES, CONV_HALO + tile, LANES), _F32),
            pltpu.VMEM((2, tile, D_FF), _BF16),
        ],
        compiler_params=pltpu.CompilerParams(
            dimension_semantics=("arbitrary",), vmem_limit_bytes=VMEM_LIMIT_BYTES),
        name="conv_glu_mlp",
    )(x2, x2, gains, w_up, conv_w, conv_b, w_down)
    return out.reshape(b, s, d)


def _ret_kernel(xc_ref, pos_ref, xd_ref, gain_ref, freq_ref, win_ref, intra_ref, cross_ref,
                indec_ref, cdec_ref, gn_ref, wout_ref, o_ref, state_ref, qkh_ref, y_ref,
                *, tile, tiles_per_row, n_tiles):
    i = pl.program_id(0)
    n_sub = tile // RET_CHUNK
    q0, k0, kd0, h0 = 0, RET_QK_TOTAL, 2 * RET_QK_TOTAL, 3 * RET_QK_TOTAL

    @pl.when((i - 1) % tiles_per_row == 0)
    def _():
        state_ref[...] = jnp.zeros(state_ref.shape, _F32)

    def output(src):
        f = _dot(y_ref[src], wout_ref[...])
        o_ref[...] = xd_ref[...] + _rms(f, gain_ref[1:2, :])

    def recurrence(src, dst):
        for hd in range(RET_HEADS):
            vc = slice(2 * RET_QK_TOTAL + hd * RET_V_DIM, 2 * RET_QK_TOTAL + (hd + 1) * RET_V_DIM)
            gc = slice(2 * RET_QK_TOTAL + RET_V_TOTAL + hd * RET_V_DIM,
                       2 * RET_QK_TOTAL + RET_V_TOTAL + (hd + 1) * RET_V_DIM)
            yc = slice(hd * RET_V_DIM, (hd + 1) * RET_V_DIM)
            hc = slice(hd * RET_QK_DIM, (hd + 1) * RET_QK_DIM)
            intra = intra_ref[hd]
            cross = cross_ref[hd][:, 0:1]
            cdec = cdec_ref[hd][0:1, 0:1]
            for sub in range(n_sub):
                rows = slice(sub * RET_CHUNK, (sub + 1) * RET_CHUNK)
                hp = qkh_ref[src, rows, h0:h0 + D_MODEL]
                qs = qkh_ref[src, rows, q0 + hc.start:q0 + hc.stop]
                ks = qkh_ref[src, rows, k0 + hc.start:k0 + hc.stop]
                kd = qkh_ref[src, rows, kd0 + hc.start:kd0 + hc.stop]
                vs = _dot(hp, win_ref[:, vc]).astype(_BF16)
                state = state_ref[hd]
                scores = lax.dot_general(qs, ks, (((1,), (1,)), ((), ())),
                                         preferred_element_type=_F32) * intra
                o = _dot(scores.astype(_BF16), vs) + _dot(qs, state.astype(_BF16)) * cross
                state_ref[hd] = state * cdec + lax.dot_general(
                    kd, vs, (((0,), (0,)), ((), ())), preferred_element_type=_F32)
                mu = jnp.mean(o, axis=-1, keepdims=True)
                dev = o - mu
                var = jnp.mean(dev * dev, axis=-1, keepdims=True)
                on = dev * lax.rsqrt(var + EPS) * gn_ref[:, yc]
                gate = _dot(hp, win_ref[:, gc])
                y_ref[dst, rows, yc] = (_silu(gate) * on).astype(_BF16)

    def projection(dst):
        h = _rms(xc_ref[...], gain_ref[0:1, :]).astype(_BF16)
        qkh_ref[dst, :, h0:h0 + D_MODEL] = h
        ang = pos_ref[...].astype(_F32) * freq_ref[...]
        cos, sin = jnp.cos(ang), jnp.sin(ang)
        q_scale = RET_QK_DIM ** -0.5
        half = RET_QK_DIM // 2

        def rotary(a, c, sn):
            a1, a2 = a[:, :half], a[:, half:]
            return jnp.concatenate([a1 * c - a2 * sn, a2 * c + a1 * sn], axis=-1)

        for hd in range(RET_HEADS):
            hc = slice(hd * RET_QK_DIM, (hd + 1) * RET_QK_DIM)
            q = rotary(_dot(h, win_ref[:, q0 + hc.start:q0 + hc.stop]), cos * q_scale, sin * q_scale)
            k = rotary(_dot(h, win_ref[:, k0 + hc.start:k0 + hc.stop]), cos, sin)
            qkh_ref[dst, :, q0 + hc.start:q0 + hc.stop] = q.astype(_BF16)
            qkh_ref[dst, :, k0 + hc.start:k0 + hc.stop] = k.astype(_BF16)
            indec = indec_ref[hd][:, 0:1]
            for sub in range(n_sub):
                rows = slice(sub * RET_CHUNK, (sub + 1) * RET_CHUNK)
                qkh_ref[dst, rows, kd0 + hc.start:kd0 + hc.stop] = (k[rows] * indec).astype(_BF16)

    def step(parity, with_proj, with_rec, with_out):
        if with_out:
            output(1 - parity)
        if with_rec:
            recurrence(1 - parity, parity)
        if with_proj:
            projection(parity)

    def zero_first_output_input():
        y_ref[0] = jnp.zeros(y_ref.shape[1:], _BF16)

    _pipeline_steps(i, n_tiles, step, zero_first_output_input)


def _retention_tables():
    log_gamma = jnp.log(1.0 - 2.0 ** (-5.0 - jnp.arange(RET_HEADS, dtype=_F32)))
    i = jnp.arange(RET_CHUNK, dtype=_F32)
    rel = i[:, None] - i[None, :]
    intra = jnp.where(rel >= 0, jnp.exp(jnp.maximum(rel, 0.0) * log_gamma[:, None, None]), 0.0)
    cross = jnp.exp((i + 1.0) * log_gamma[:, None])
    indec = jnp.exp((RET_CHUNK - 1.0 - i) * log_gamma[:, None])
    cdec = jnp.exp(RET_CHUNK * log_gamma)
    lanes = 128
    rep = lambda a: jnp.broadcast_to(a[:, :, None], a.shape + (lanes,))
    cdec = jnp.broadcast_to(cdec[:, None, None], (RET_HEADS, 8, lanes))
    inv_freq = ROPE_BASE ** (-jnp.arange(0, RET_QK_DIM, 2, dtype=_F32) / RET_QK_DIM)
    return inv_freq[None, :], intra, rep(cross), rep(indec), cdec


def _retention_layer(x, positions, gains, w_in, gn_gain, w_out):
    b, s, d = x.shape
    tile = SEQ_TILE
    n_tiles = b * s // tile
    inv_freq, intra, cross, indec, cdec = _retention_tables()
    const2 = lambda *_: (0, 0)
    const3 = lambda *_: (0, 0, 0)
    once2 = functools.partial(pl.BlockSpec, index_map=const2, pipeline_mode=pl.Buffered(1))
    once3 = functools.partial(pl.BlockSpec, index_map=const3, pipeline_mode=pl.Buffered(1))
    cur = lambda i: (jnp.minimum(i, n_tiles - 1), 0)
    done = lambda i: (jnp.maximum(i - 2, 0), 0)
    x2 = x.reshape(b * s, d)
    out = pl.pallas_call(
        functools.partial(_ret_kernel, tile=tile, tiles_per_row=s // tile, n_tiles=n_tiles),
        grid=(n_tiles + 2,),
        in_specs=[
            pl.BlockSpec((tile, d), cur),
            pl.BlockSpec((tile, 1), cur),
            pl.BlockSpec((tile, d), done),
            pl.BlockSpec(gains.shape, const2),
            pl.BlockSpec(inv_freq.shape, const2),
            once2(w_in.shape),
            once3(intra.shape),
            once3(cross.shape),
            once3(indec.shape),
            once3(cdec.shape),
            pl.BlockSpec(gn_gain.shape, const2),
            once2(w_out.shape),
        ],
        out_specs=pl.BlockSpec((tile, d), done),
        out_shape=jax.ShapeDtypeStruct(x2.shape, x.dtype),
        scratch_shapes=[
            pltpu.VMEM((RET_HEADS, RET_QK_DIM, RET_V_DIM), _F32),
            pltpu.VMEM((2, tile, 3 * RET_QK_TOTAL + D_MODEL), _BF16),
            pltpu.VMEM((2, tile, RET_V_TOTAL), _BF16),
        ],
        compiler_params=pltpu.CompilerParams(
            dimension_semantics=("arbitrary",), vmem_limit_bytes=VMEM_LIMIT_BYTES),
        name="retention",
    )(x2, positions.reshape(b * s, 1), x2, gains, inv_freq, w_in, intra, cross, indec, cdec,
      gn_gain, w_out)
    return out.reshape(b, s, d)


def kernel(x, positions, norm_gain, pool_w, pool_scale, ret_w_in, ret_gn_gain, ret_w_out,
           mlp_w_up, mlp_conv_w, mlp_conv_b, mlp_w_down):
    bf = lambda a: a.astype(_BF16)
    w_up, w_down = bf(mlp_w_up), bf(mlp_w_down)
    x = _pool_layer(x, norm_gain[0], bf(pool_w[0]), pool_scale[0][None, :])
    x = _mlp_layer(x, norm_gain[0], 0, w_up, mlp_conv_w[0], mlp_conv_b[0][None, :], w_down)
    x = _retention_layer(x, positions, norm_gain[1], bf(ret_w_in[0]), ret_gn_gain[0][None, :],
                         bf(ret_w_out[0]))
    x = _mlp_layer(x, norm_gain[1], 1, w_up, mlp_conv_w[1], mlp_conv_b[1][None, :], w_down)
    return x
```

```python
import functools

import jax
import jax.numpy as jnp
from jax import lax
from jax.experimental import pallas as pl
from jax.experimental.pallas import tpu as pltpu

LANES = 128
D_MODEL = 1024
POOL_WINDOWS = (2, 4, 8, 16)
POOL_GROUP_DIM = D_MODEL // len(POOL_WINDOWS)
POOL_HALO = 16
RET_HEADS = 4
RET_QK_DIM = 256
RET_V_DIM = 512
RET_QK_TOTAL = RET_HEADS * RET_QK_DIM
RET_V_TOTAL = RET_HEADS * RET_V_DIM
RET_CHUNK = 256
ROPE_BASE = 10000.0
D_FF = 2816
FF_CHUNK = 256
UP_ROWS = 128
GATE_LEAD = 6
POOL_GATE_LEAD = 2
CONV_HALO = 8
EPS = 1e-6

SEQ_TILE = 512
VMEM_LIMIT_BYTES = 60 * 1024 * 1024

_F32 = jnp.float32
_BF16 = jnp.bfloat16


def _dot(a, b):
    return jnp.dot(a, b, preferred_element_type=_F32)


def _rms(x, gain):
    var = jnp.mean(x * x, axis=-1, keepdims=True)
    return x * lax.rsqrt(var + EPS) * gain


def _silu(x):
    return x * (1.0 / (1.0 + jnp.exp(-x)))


def _pipeline_steps(i, n_tiles, step, init_last_stage_input):
    @pl.when(i == 0)
    def _():
        init_last_stage_input()
        step(0, True, False, False)

    for parity in (0, 1):
        steady = jnp.logical_and(jnp.logical_and(i >= 1, i <= n_tiles), i % 2 == parity)
        pl.when(steady)(functools.partial(step, parity, True, True, True))
    pl.when(i == n_tiles + 1)(lambda: step((n_tiles + 1) % 2, False, False, True))


def _pool_tile(x, t0, gain_ref, w_ref, scale_ref, ext_ref, tile):
    h = _rms(x, gain_ref[0:1, :])
    t = t0 + lax.broadcasted_iota(jnp.int32, (tile, 1), 0)
    parts = []
    for g, win in enumerate(POOL_WINDOWS):
        inv_count = 1.0 / jnp.minimum(t + 1, win).astype(_F32)
        diffs = []
        for k in range(POOL_GROUP_DIM // LANES):
            lo = g * POOL_GROUP_DIM + k * LANES
            slab = ext_ref.at[lo // LANES]
            hk = h[:, lo:lo + LANES]
            slab[POOL_HALO:POOL_HALO + tile, :] = hk
            acc = hk
            for j in range(1, win):
                acc = acc + slab[POOL_HALO - j:POOL_HALO - j + tile, :]
            slab[0:POOL_HALO, :] = hk[tile - POOL_HALO:, :]
            diffs.append(acc * inv_count - hk)
        diff = jnp.concatenate(diffs, axis=-1)
        parts.append(_dot(diff.astype(_BF16), w_ref[g]))
    y = jnp.concatenate(parts, axis=-1) * scale_ref[...]
    return x + _rms(y, gain_ref[1:2, :])


def _mlp_kernel(*refs, tile, tiles_per_row, n_tiles, pool):
    if pool:
        (xc_ref, gain_ref, pw_ref, ps_ref, wup_ref, cw_ref, cb_ref, wdn_ref, o_ref,
         u_ref, a_ref, ext_ref, x1_ref) = refs
    else:
        xc_ref, xp_ref, gain_ref, wup_ref, cw_ref, cb_ref, wdn_ref, o_ref, u_ref, a_ref = refs
    i = pl.program_id(0)
    n_chunks = D_FF // FF_CHUNK
    lead_steady = POOL_GATE_LEAD if pool else GATE_LEAD

    @pl.when((i - 1) % tiles_per_row == 0)
    def _():
        u_ref[:, 0:CONV_HALO, :] = jnp.zeros((2 * D_FF // LANES, CONV_HALO, LANES), _F32)

    if pool:
        @pl.when(i % tiles_per_row == 0)
        def _():
            ext_ref[:, 0:POOL_HALO, :] = jnp.zeros((D_MODEL // LANES, POOL_HALO, LANES), _F32)

    def down(slot):
        f = _dot(a_ref[slot], wdn_ref[...])
        residual = x1_ref[2] if pool else xp_ref[...]
        o_ref[...] = residual + _rms(f, gain_ref[3:4, :])

    def mlp_input():
        if not pool:
            return xc_ref[...]
        x1 = _pool_tile(xc_ref[...], (i % tiles_per_row) * tile, gain_ref, pw_ref, ps_ref,
                        ext_ref, tile)
        x1_ref[0] = x1
        return x1

    def project(h, c):
        for r0 in range(0, tile, UP_ROWS):
            for col in (c * FF_CHUNK, D_FF + c * FF_CHUNK):
                u = _dot(h[r0:r0 + UP_ROWS], wup_ref[:, col:col + FF_CHUNK])
                for k in range(FF_CHUNK // LANES):
                    slab = u_ref.at[col // LANES + k]
                    slab[CONV_HALO + r0:CONV_HALO + r0 + UP_ROWS, :] = u[:, k * LANES:(k + 1) * LANES]

    def conv(col):
        parts = []
        for k in range(FF_CHUNK // LANES):
            cols = slice(col + k * LANES, col + (k + 1) * LANES)
            slab = u_ref.at[cols.start // LANES]
            taps = [slab[CONV_HALO - 2 + j:CONV_HALO - 2 + j + tile, :] * cw_ref[j:j + 1, cols]
                    for j in range(3)]
            parts.append(cb_ref[:, cols] + taps[0] + taps[1] + taps[2])
            slab[0:CONV_HALO, :] = slab[tile:tile + CONV_HALO, :]
        return jnp.concatenate(parts, axis=-1)

    def gated(slot, c):
        gate = conv(c * FF_CHUNK)
        val = conv(D_FF + c * FF_CHUNK)
        a_ref[slot, :, c * FF_CHUNK:(c + 1) * FF_CHUNK] = (_silu(gate) * val).astype(_BF16)

    def step(parity, with_up, with_gate, with_down):
        if pool and (with_down or with_gate):
            x1_ref[2] = x1_ref[1]
            x1_ref[1] = x1_ref[0]
        h = _rms(mlp_input(), gain_ref[2:3, :]).astype(_BF16) if with_up else None
        lead = lead_steady if with_up else n_chunks
        if with_gate:
            for c in range(lead):
                gated(parity, c)
        if with_down:
            down(1 - parity)
        if with_up:
            for c in range(n_chunks):
                project(h, c)
                if with_gate and c + lead < n_chunks:
                    gated(parity, c + lead)

    def zero_first_down_input():
        a_ref[0] = jnp.zeros(a_ref.shape[1:], _BF16)
        if pool:
            x1_ref[1] = jnp.zeros(x1_ref.shape[1:], _F32)

    _pipeline_steps(i, n_tiles, step, zero_first_down_input)


def _mlp_layer(x, gains, layer, w_up, conv_w, conv_b, w_down, pool_params=None):
    b, s, d = x.shape
    tile = SEQ_TILE
    n_tiles = b * s // tile
    pool = pool_params is not None
    const = lambda *_: (0, 0)
    resident = lambda a: pl.BlockSpec((None,) + a.shape[1:], lambda *_: (layer, 0, 0),
                                      pipeline_mode=pl.Buffered(1))
    cur = lambda i: (jnp.minimum(i, n_tiles - 1), 0)
    done = lambda i: (jnp.maximum(i - 2, 0), 0)
    x2 = x.reshape(b * s, d)
    mlp_specs = [resident(w_up), pl.BlockSpec(conv_w.shape, const), pl.BlockSpec(conv_b.shape, const),
                 resident(w_down)]
    mlp_args = (w_up, conv_w, conv_b, w_down)
    scratch = [pltpu.VMEM((2 * D_FF // LANES, CONV_HALO + tile, LANES), _F32),
               pltpu.VMEM((2, tile, D_FF), _BF16)]
    if pool:
        pool_w, pool_scale = pool_params
        in_specs = [pl.BlockSpec((tile, d), cur), pl.BlockSpec(gains.shape, const),
                    pl.BlockSpec(pool_w.shape, lambda *_: (0, 0, 0), pipeline_mode=pl.Buffered(1)),
                    pl.BlockSpec(pool_scale.shape, const)] + mlp_specs
        args = (x2, gains, pool_w, pool_scale) + mlp_args
        scratch += [pltpu.VMEM((d // LANES, POOL_HALO + tile, LANES), _F32),
                    pltpu.VMEM((3, tile, d), _F32)]
    else:
        in_specs = [pl.BlockSpec((tile, d), cur), pl.BlockSpec((tile, d), done),
                    pl.BlockSpec(gains.shape, const)] + mlp_specs
        args = (x2, x2, gains) + mlp_args
    out = pl.pallas_call(
        functools.partial(_mlp_kernel, tile=tile, tiles_per_row=s // tile, n_tiles=n_tiles,
                          pool=pool),
        grid=(n_tiles + 2,),
        in_specs=in_specs,
        out_specs=pl.BlockSpec((tile, d), done),
        out_shape=jax.ShapeDtypeStruct(x2.shape, x.dtype),
        scratch_shapes=scratch,
        compiler_params=pltpu.CompilerParams(
            dimension_semantics=("arbitrary",), vmem_limit_bytes=VMEM_LIMIT_BYTES),
        name="pool_conv_glu_mlp" if pool else "conv_glu_mlp",
    )(*args)
    return out.reshape(b, s, d)


def _ret_kernel(xc_ref, pos_ref, xd_ref, gain_ref, freq_ref, win_ref, intra_ref, cross_ref,
                indec_ref, cdec_ref, gn_ref, wout_ref, o_ref, state_ref, qkh_ref, y_ref,
                *, tile, tiles_per_row, n_tiles):
    i = pl.program_id(0)
    n_sub = tile // RET_CHUNK
    q0, k0, kd0, h0 = 0, RET_QK_TOTAL, 2 * RET_QK_TOTAL, 3 * RET_QK_TOTAL

    @pl.when((i - 1) % tiles_per_row == 0)
    def _():
        state_ref[...] = jnp.zeros(state_ref.shape, _F32)

    def output(src):
        f = _dot(y_ref[src], wout_ref[...])
        o_ref[...] = xd_ref[...] + _rms(f, gain_ref[1:2, :])

    def recurrence(src, dst):
        for hd in range(RET_HEADS):
            vc = slice(2 * RET_QK_TOTAL + hd * RET_V_DIM, 2 * RET_QK_TOTAL + (hd + 1) * RET_V_DIM)
            gc = slice(2 * RET_QK_TOTAL + RET_V_TOTAL + hd * RET_V_DIM,
                       2 * RET_QK_TOTAL + RET_V_TOTAL + (hd + 1) * RET_V_DIM)
            yc = slice(hd * RET_V_DIM, (hd + 1) * RET_V_DIM)
            hc = slice(hd * RET_QK_DIM, (hd + 1) * RET_QK_DIM)
            intra = intra_ref[hd]
            cross = cross_ref[hd][:, 0:1]
            cdec = cdec_ref[hd][0:1, 0:1]
            for sub in range(n_sub):
                rows = slice(sub * RET_CHUNK, (sub + 1) * RET_CHUNK)
                hp = qkh_ref[src, rows, h0:h0 + D_MODEL]
                qs = qkh_ref[src, rows, q0 + hc.start:q0 + hc.stop]
                ks = qkh_ref[src, rows, k0 + hc.start:k0 + hc.stop]
                kd = qkh_ref[src, rows, kd0 + hc.start:kd0 + hc.stop]
                vs = _dot(hp, win_ref[:, vc]).astype(_BF16)
                state = state_ref[hd]
                scores = lax.dot_general(qs, ks, (((1,), (1,)), ((), ())),
                                         preferred_element_type=_F32) * intra
                o = _dot(scores.astype(_BF16), vs) + _dot(qs, state.astype(_BF16)) * cross
                state_ref[hd] = state * cdec + lax.dot_general(
                    kd, vs, (((0,), (0,)), ((), ())), preferred_element_type=_F32)
                mu = jnp.mean(o, axis=-1, keepdims=True)
                dev = o - mu
                var = jnp.mean(dev * dev, axis=-1, keepdims=True)
                on = dev * lax.rsqrt(var + EPS) * gn_ref[:, yc]
                gate = _dot(hp, win_ref[:, gc])
                y_ref[dst, rows, yc] = (_silu(gate) * on).astype(_BF16)

    def projection(dst):
        h = _rms(xc_ref[...], gain_ref[0:1, :]).astype(_BF16)
        qkh_ref[dst, :, h0:h0 + D_MODEL] = h
        ang = pos_ref[...].astype(_F32) * freq_ref[...]
        cos, sin = jnp.cos(ang), jnp.sin(ang)
        q_scale = RET_QK_DIM ** -0.5
        half = RET_QK_DIM // 2

        def rotary(a, c, sn):
            a1, a2 = a[:, :half], a[:, half:]
            return jnp.concatenate([a1 * c - a2 * sn, a2 * c + a1 * sn], axis=-1)

        for hd in range(RET_HEADS):
            hc = slice(hd * RET_QK_DIM, (hd + 1) * RET_QK_DIM)
            q = rotary(_dot(h, win_ref[:, q0 + hc.start:q0 + hc.stop]), cos * q_scale, sin * q_scale)
            k = rotary(_dot(h, win_ref[:, k0 + hc.start:k0 + hc.stop]), cos, sin)
            qkh_ref[dst, :, q0 + hc.start:q0 + hc.stop] = q.astype(_BF16)
            qkh_ref[dst, :, k0 + hc.start:k0 + hc.stop] = k.astype(_BF16)
            indec = indec_ref[hd][:, 0:1]
            for sub in range(n_sub):
                rows = slice(sub * RET_CHUNK, (sub + 1) * RET_CHUNK)
                qkh_ref[dst, rows, kd0 + hc.start:kd0 + hc.stop] = (k[rows] * indec).astype(_BF16)

    def step(parity, with_proj, with_rec, with_out):
        if with_out:
            output(1 - parity)
        if with_rec:
            recurrence(1 - parity, parity)
        if with_proj:
            projection(parity)

    def zero_first_output_input():
        y_ref[0] = jnp.zeros(y_ref.shape[1:], _BF16)

    _pipeline_steps(i, n_tiles, step, zero_first_output_input)


def _retention_tables():
    log_gamma = jnp.log(1.0 - 2.0 ** (-5.0 - jnp.arange(RET_HEADS, dtype=_F32)))
    i = jnp.arange(RET_CHUNK, dtype=_F32)
    rel = i[:, None] - i[None, :]
    intra = jnp.where(rel >= 0, jnp.exp(jnp.maximum(rel, 0.0) * log_gamma[:, None, None]), 0.0)
    cross = jnp.exp((i + 1.0) * log_gamma[:, None])
    indec = jnp.exp((RET_CHUNK - 1.0 - i) * log_gamma[:, None])
    cdec = jnp.exp(RET_CHUNK * log_gamma)
    lanes = 128
    rep = lambda a: jnp.broadcast_to(a[:, :, None], a.shape + (lanes,))
    cdec = jnp.broadcast_to(cdec[:, None, None], (RET_HEADS, 8, lanes))
    inv_freq = ROPE_BASE ** (-jnp.arange(0, RET_QK_DIM, 2, dtype=_F32) / RET_QK_DIM)
    return inv_freq[None, :], intra, rep(cross), rep(indec), cdec


def _retention_layer(x, positions, gains, w_in, gn_gain, w_out):
    b, s, d = x.shape
    tile = SEQ_TILE
    n_tiles = b * s // tile
    inv_freq, intra, cross, indec, cdec = _retention_tables()
    const2 = lambda *_: (0, 0)
    const3 = lambda *_: (0, 0, 0)
    once2 = functools.partial(pl.BlockSpec, index_map=const2, pipeline_mode=pl.Buffered(1))
    once3 = functools.partial(pl.BlockSpec, index_map=const3, pipeline_mode=pl.Buffered(1))
    cur = lambda i: (jnp.minimum(i, n_tiles - 1), 0)
    done = lambda i: (jnp.maximum(i - 2, 0), 0)
    x2 = x.reshape(b * s, d)
    out = pl.pallas_call(
        functools.partial(_ret_kernel, tile=tile, tiles_per_row=s // tile, n_tiles=n_tiles),
        grid=(n_tiles + 2,),
        in_specs=[
            pl.BlockSpec((tile, d), cur),
            pl.BlockSpec((tile, 1), cur),
            pl.BlockSpec((tile, d), done),
            pl.BlockSpec(gains.shape, const2),
            pl.BlockSpec(inv_freq.shape, const2),
            once2(w_in.shape),
            once3(intra.shape),
            once3(cross.shape),
            once3(indec.shape),
            once3(cdec.shape),
            pl.BlockSpec(gn_gain.shape, const2),
            once2(w_out.shape),
        ],
        out_specs=pl.BlockSpec((tile, d), done),
        out_shape=jax.ShapeDtypeStruct(x2.shape, x.dtype),
        scratch_shapes=[
            pltpu.VMEM((RET_HEADS, RET_QK_DIM, RET_V_DIM), _F32),
            pltpu.VMEM((2, tile, 3 * RET_QK_TOTAL + D_MODEL), _BF16),
            pltpu.VMEM((2, tile, RET_V_TOTAL), _BF16),
        ],
        compiler_params=pltpu.CompilerParams(
            dimension_semantics=("arbitrary",), vmem_limit_bytes=VMEM_LIMIT_BYTES),
        name="retention",
    )(x2, positions.reshape(b * s, 1), x2, gains, inv_freq, w_in, intra, cross, indec, cdec,
      gn_gain, w_out)
    return out.reshape(b, s, d)


def kernel(x, positions, norm_gain, pool_w, pool_scale, ret_w_in, ret_gn_gain, ret_w_out,
           mlp_w_up, mlp_conv_w, mlp_conv_b, mlp_w_down):
    bf = lambda a: a.astype(_BF16)
    w_up, w_down = bf(mlp_w_up), bf(mlp_w_down)
    x = _mlp_layer(x, norm_gain[0], 0, w_up, mlp_conv_w[0], mlp_conv_b[0][None, :], w_down,
                   pool_params=(bf(pool_w[0]), pool_scale[0][None, :]))
    x = _retention_layer(x, positions, norm_gain[1], bf(ret_w_in[0]), ret_gn_gain[0][None, :],
                         bf(ret_w_out[0]))
    x = _mlp_layer(x, norm_gain[1], 1, w_up, mlp_conv_w[1], mlp_conv_b[1][None, :], w_down)
    return x
```

```python
import functools

import jax
import jax.numpy as jnp
from jax import lax
from jax.experimental import pallas as pl
from jax.experimental.pallas import tpu as pltpu

LANES = 128
D_MODEL = 1024
POOL_WINDOWS = (2, 4, 8, 16)
POOL_GROUP_DIM = D_MODEL // len(POOL_WINDOWS)
POOL_HALO = 16
RET_HEADS = 4
RET_QK_DIM = 256
RET_V_DIM = 512
RET_QK_TOTAL = RET_HEADS * RET_QK_DIM
RET_V_TOTAL = RET_HEADS * RET_V_DIM
RET_CHUNK = 256
ROPE_BASE = 10000.0
D_FF = 2816
FF_CHUNK = 256
UP_ROWS = 128
GATE_LEAD = 6
POOL_GATE_LEAD = 2
CONV_HALO = 8
EPS = 1e-6

SEQ_TILE = 512
VMEM_LIMIT_BYTES = 60 * 1024 * 1024

_F32 = jnp.float32
_BF16 = jnp.bfloat16


def _dot(a, b):
    return jnp.dot(a, b, preferred_element_type=_F32)


def _rms(x, gain):
    var = jnp.mean(x * x, axis=-1, keepdims=True)
    return x * lax.rsqrt(var + EPS) * gain


def _silu(x):
    return x * (1.0 / (1.0 + jnp.exp(-x)))


def _pipeline_steps(i, n_tiles, step, init_last_stage_input):
    @pl.when(i == 0)
    def _():
        init_last_stage_input()
        step(0, True, False, False)

    for parity in (0, 1):
        steady = jnp.logical_and(jnp.logical_and(i >= 1, i <= n_tiles), i % 2 == parity)
        pl.when(steady)(functools.partial(step, parity, True, True, True))
    pl.when(i == n_tiles + 1)(lambda: step((n_tiles + 1) % 2, False, False, True))


def _pool_tile(x, t0, gain_ref, w_ref, scale_ref, ext_ref, tile):
    h = _rms(x, gain_ref[0:1, :])
    t = t0 + lax.broadcasted_iota(jnp.int32, (tile, 1), 0)
    parts = []
    for g, win in enumerate(POOL_WINDOWS):
        inv_count = 1.0 / jnp.minimum(t + 1, win).astype(_F32)
        diffs = []
        for k in range(POOL_GROUP_DIM // LANES):
            lo = g * POOL_GROUP_DIM + k * LANES
            slab = ext_ref.at[lo // LANES]
            hk = h[:, lo:lo + LANES]
            slab[POOL_HALO:POOL_HALO + tile, :] = hk
            acc = hk
            for j in range(1, win):
                acc = acc + slab[POOL_HALO - j:POOL_HALO - j + tile, :]
            slab[0:POOL_HALO, :] = hk[tile - POOL_HALO:, :]
            diffs.append(acc * inv_count - hk)
        diff = jnp.concatenate(diffs, axis=-1)
        parts.append(_dot(diff.astype(_BF16), w_ref[g]))
    y = jnp.concatenate(parts, axis=-1) * scale_ref[...]
    return x + _rms(y, gain_ref[1:2, :])


def _mlp_kernel(*refs, tile, tiles_per_row, n_tiles, pool):
    if pool:
        (xc_ref, gain_ref, pw_ref, ps_ref, wup_ref, cw_ref, cb_ref, wdn_ref, o_ref,
         u_ref, a_ref, ext_ref, x1_ref) = refs
    else:
        xc_ref, xp_ref, gain_ref, wup_ref, cw_ref, cb_ref, wdn_ref, o_ref, u_ref, a_ref = refs
    i = pl.program_id(0)
    n_chunks = D_FF // FF_CHUNK
    lead_steady = POOL_GATE_LEAD if pool else GATE_LEAD

    @pl.when((i - 1) % tiles_per_row == 0)
    def _():
        u_ref[:, 0:CONV_HALO, :] = jnp.zeros((2 * D_FF // LANES, CONV_HALO, LANES), _F32)

    if pool:
        @pl.when(i % tiles_per_row == 0)
        def _():
            ext_ref[:, 0:POOL_HALO, :] = jnp.zeros((D_MODEL // LANES, POOL_HALO, LANES), _F32)

    def down(slot):
        f = _dot(a_ref[slot], wdn_ref[...])
        residual = x1_ref[2] if pool else xp_ref[...]
        o_ref[...] = residual + _rms(f, gain_ref[3:4, :])

    def mlp_input():
        if not pool:
            return xc_ref[...]
        x1 = _pool_tile(xc_ref[...], (i % tiles_per_row) * tile, gain_ref, pw_ref, ps_ref,
                        ext_ref, tile)
        x1_ref[0] = x1
        return x1

    def project(h, c):
        for r0 in range(0, tile, UP_ROWS):
            for col in (c * FF_CHUNK, D_FF + c * FF_CHUNK):
                u = _dot(h[r0:r0 + UP_ROWS], wup_ref[:, col:col + FF_CHUNK])
                for k in range(FF_CHUNK // LANES):
                    slab = u_ref.at[col // LANES + k]
                    slab[CONV_HALO + r0:CONV_HALO + r0 + UP_ROWS, :] = u[:, k * LANES:(k + 1) * LANES]

    def conv(col):
        parts = []
        for k in range(FF_CHUNK // LANES):
            cols = slice(col + k * LANES, col + (k + 1) * LANES)
            slab = u_ref.at[cols.start // LANES]
            taps = [slab[CONV_HALO - 2 + j:CONV_HALO - 2 + j + tile, :] * cw_ref[j:j + 1, cols]
                    for j in range(3)]
            parts.append(cb_ref[:, cols] + taps[0] + taps[1] + taps[2])
            slab[0:CONV_HALO, :] = slab[tile:tile + CONV_HALO, :]
        return jnp.concatenate(parts, axis=-1)

    def gated(slot, c):
        gate = conv(c * FF_CHUNK)
        val = conv(D_FF + c * FF_CHUNK)
        a_ref[slot, :, c * FF_CHUNK:(c + 1) * FF_CHUNK] = (_silu(gate) * val).astype(_BF16)

    def step(parity, with_up, with_gate, with_down):
        if pool and (with_down or with_gate):
            x1_ref[2] = x1_ref[1]
            x1_ref[1] = x1_ref[0]
        h = _rms(mlp_input(), gain_ref[2:3, :]).astype(_BF16) if with_up else None
        lead = lead_steady if with_up else n_chunks
        if with_gate:
            for c in range(lead):
                gated(parity, c)
        if with_down:
            down(1 - parity)
        if with_up:
            for c in range(n_chunks):
                project(h, c)
                if with_gate and c + lead < n_chunks:
                    gated(parity, c + lead)

    def zero_first_down_input():
        a_ref[0] = jnp.zeros(a_ref.shape[1:], _BF16)
        if pool:
            x1_ref[1] = jnp.zeros(x1_ref.shape[1:], _F32)

    _pipeline_steps(i, n_tiles, step, zero_first_down_input)


def _mlp_layer(x, gains, layer, w_up, conv_w, conv_b, w_down, pool_params=None):
    b, s, d = x.shape
    tile = SEQ_TILE
    n_tiles = b * s // tile
    pool = pool_params is not None
    const = lambda *_: (0, 0)
    resident = lambda a: pl.BlockSpec((None,) + a.shape[1:], lambda *_: (layer, 0, 0),
                                      pipeline_mode=pl.Buffered(1))
    cur = lambda i: (jnp.minimum(i, n_tiles - 1), 0)
    done = lambda i: (jnp.maximum(i - 2, 0), 0)
    x2 = x.reshape(b * s, d)
    mlp_specs = [resident(w_up), pl.BlockSpec(conv_w.shape, const), pl.BlockSpec(conv_b.shape, const),
                 resident(w_down)]
    mlp_args = (w_up, conv_w, conv_b, w_down)
    scratch = [pltpu.VMEM((2 * D_FF // LANES, CONV_HALO + tile, LANES), _F32),
               pltpu.VMEM((2, tile, D_FF), _BF16)]
    if pool:
        pool_w, pool_scale = pool_params
        in_specs = [pl.BlockSpec((tile, d), cur), pl.BlockSpec(gains.shape, const),
                    pl.BlockSpec(pool_w.shape, lambda *_: (0, 0, 0), pipeline_mode=pl.Buffered(1)),
                    pl.BlockSpec(pool_scale.shape, const)] + mlp_specs
        args = (x2, gains, pool_w, pool_scale) + mlp_args
        scratch += [pltpu.VMEM((d // LANES, POOL_HALO + tile, LANES), _F32),
                    pltpu.VMEM((3, tile, d), _F32)]
    else:
        in_specs = [pl.BlockSpec((tile, d), cur), pl.BlockSpec((tile, d), done),
                    pl.BlockSpec(gains.shape, const)] + mlp_specs
        args = (x2, x2, gains) + mlp_args
    out = pl.pallas_call(
        functools.partial(_mlp_kernel, tile=tile, tiles_per_row=s // tile, n_tiles=n_tiles,
                          pool=pool),
        grid=(n_tiles + 2,),
        in_specs=in_specs,
        out_specs=pl.BlockSpec((tile, d), done),
        out_shape=jax.ShapeDtypeStruct(x2.shape, x.dtype),
        scratch_shapes=scratch,
        compiler_params=pltpu.CompilerParams(
            dimension_semantics=("arbitrary",), vmem_limit_bytes=VMEM_LIMIT_BYTES),
        name="pool_conv_glu_mlp" if pool else "conv_glu_mlp",
    )(*args)
    return out.reshape(b, s, d)


def _ret_kernel(xc_ref, pos_ref, xd_ref, gain_ref, freq_ref, win_ref, intra_ref, cross_ref,
                indec_ref, cdec_ref, gn_ref, wout_ref, o_ref, state_ref, qkh_ref, y_ref,
                *, tile, tiles_per_row, n_tiles):
    i = pl.program_id(0)
    n_sub = tile // RET_CHUNK
    q0, k0, kd0, h0 = 0, RET_QK_TOTAL, 2 * RET_QK_TOTAL, 3 * RET_QK_TOTAL

    @pl.when((i - 1) % tiles_per_row == 0)
    def _():
        state_ref[...] = jnp.zeros(state_ref.shape, _F32)

    def output(src):
        f = _dot(y_ref[src], wout_ref[...])
        o_ref[...] = xd_ref[...] + _rms(f, gain_ref[1:2, :])

    def recur_matmuls(src, hd, sub):
        vc = slice(2 * RET_QK_TOTAL + hd * RET_V_DIM, 2 * RET_QK_TOTAL + (hd + 1) * RET_V_DIM)
        gc = slice(2 * RET_QK_TOTAL + RET_V_TOTAL + hd * RET_V_DIM,
                   2 * RET_QK_TOTAL + RET_V_TOTAL + (hd + 1) * RET_V_DIM)
        hc = slice(hd * RET_QK_DIM, (hd + 1) * RET_QK_DIM)
        intra = intra_ref[hd]
        cross = cross_ref[hd][:, 0:1]
        cdec = cdec_ref[hd][0:1, 0:1]
        rows = slice(sub * RET_CHUNK, (sub + 1) * RET_CHUNK)
        hp = qkh_ref[src, rows, h0:h0 + D_MODEL]
        qs = qkh_ref[src, rows, q0 + hc.start:q0 + hc.stop]
        ks = qkh_ref[src, rows, k0 + hc.start:k0 + hc.stop]
        kd = qkh_ref[src, rows, kd0 + hc.start:kd0 + hc.stop]
        vs = _dot(hp, win_ref[:, vc]).astype(_BF16)
        state = state_ref[hd]
        scores = lax.dot_general(qs, ks, (((1,), (1,)), ((), ())),
                                 preferred_element_type=_F32) * intra
        o = _dot(scores.astype(_BF16), vs) + _dot(qs, state.astype(_BF16)) * cross
        state_ref[hd] = state * cdec + lax.dot_general(
            kd, vs, (((0,), (0,)), ((), ())), preferred_element_type=_F32)
        return o, hp, gc

    def recur_finish(dst, hd, sub, o, hp, gc):
        yc = slice(hd * RET_V_DIM, (hd + 1) * RET_V_DIM)
        rows = slice(sub * RET_CHUNK, (sub + 1) * RET_CHUNK)
        mu = jnp.mean(o, axis=-1, keepdims=True)
        dev = o - mu
        var = jnp.mean(dev * dev, axis=-1, keepdims=True)
        on = dev * lax.rsqrt(var + EPS) * gn_ref[:, yc]
        gate = _dot(hp, win_ref[:, gc])
        y_ref[dst, rows, yc] = (_silu(gate) * on).astype(_BF16)

    def recurrence(src, dst):
        for hd in range(RET_HEADS):
            for sub in range(n_sub):
                recur_finish(dst, hd, sub, *recur_matmuls(src, hd, sub))

    def projection_prep(dst):
        h = _rms(xc_ref[...], gain_ref[0:1, :]).astype(_BF16)
        qkh_ref[dst, :, h0:h0 + D_MODEL] = h
        pos_t = pos_ref[...].astype(_F32).T
        first = dst * (tile // LANES)
        ang = jnp.concatenate(
            [pos_t[:, first + j:first + j + 1] * freq_ref[...] for j in range(tile // LANES)],
            axis=0)
        return h, jnp.cos(ang), jnp.sin(ang)

    def projection_head(dst, hd, h, cos, sin):
        q_scale = RET_QK_DIM ** -0.5
        half = RET_QK_DIM // 2

        def rotary(a, c, sn):
            a1, a2 = a[:, :half], a[:, half:]
            return jnp.concatenate([a1 * c - a2 * sn, a2 * c + a1 * sn], axis=-1)

        hc = slice(hd * RET_QK_DIM, (hd + 1) * RET_QK_DIM)
        q = rotary(_dot(h, win_ref[:, q0 + hc.start:q0 + hc.stop]), cos * q_scale, sin * q_scale)
        k = rotary(_dot(h, win_ref[:, k0 + hc.start:k0 + hc.stop]), cos, sin)
        qkh_ref[dst, :, q0 + hc.start:q0 + hc.stop] = q.astype(_BF16)
        qkh_ref[dst, :, k0 + hc.start:k0 + hc.stop] = k.astype(_BF16)
        indec = indec_ref[hd][:, 0:1]
        for sub in range(n_sub):
            rows = slice(sub * RET_CHUNK, (sub + 1) * RET_CHUNK)
            qkh_ref[dst, rows, kd0 + hc.start:kd0 + hc.stop] = (k[rows] * indec).astype(_BF16)

    def step(parity, with_proj, with_rec, with_out):
        prep = projection_prep(parity) if with_proj else None
        if with_out:
            output(1 - parity)
        if with_rec:
            recurrence(1 - parity, parity)
        for hd in range(RET_HEADS):
            if with_proj:
                projection_head(parity, hd, *prep)

    def zero_first_output_input():
        y_ref[0] = jnp.zeros(y_ref.shape[1:], _BF16)

    _pipeline_steps(i, n_tiles, step, zero_first_output_input)


def _retention_tables():
    log_gamma = jnp.log(1.0 - 2.0 ** (-5.0 - jnp.arange(RET_HEADS, dtype=_F32)))
    i = jnp.arange(RET_CHUNK, dtype=_F32)
    rel = i[:, None] - i[None, :]
    intra = jnp.where(rel >= 0, jnp.exp(jnp.maximum(rel, 0.0) * log_gamma[:, None, None]), 0.0)
    cross = jnp.exp((i + 1.0) * log_gamma[:, None])
    indec = jnp.exp((RET_CHUNK - 1.0 - i) * log_gamma[:, None])
    cdec = jnp.exp(RET_CHUNK * log_gamma)
    lanes = 128
    rep = lambda a: jnp.broadcast_to(a[:, :, None], a.shape + (lanes,))
    cdec = jnp.broadcast_to(cdec[:, None, None], (RET_HEADS, 8, lanes))
    inv_freq = ROPE_BASE ** (-jnp.arange(0, RET_QK_DIM, 2, dtype=_F32) / RET_QK_DIM)
    return inv_freq[None, :], intra, rep(cross), rep(indec), cdec


def _retention_layer(x, positions, gains, w_in, gn_gain, w_out):
    b, s, d = x.shape
    tile = SEQ_TILE
    n_tiles = b * s // tile
    inv_freq, intra, cross, indec, cdec = _retention_tables()
    const2 = lambda *_: (0, 0)
    const3 = lambda *_: (0, 0, 0)
    once2 = functools.partial(pl.BlockSpec, index_map=const2, pipeline_mode=pl.Buffered(1))
    once3 = functools.partial(pl.BlockSpec, index_map=const3, pipeline_mode=pl.Buffered(1))
    cur = lambda i: (jnp.minimum(i, n_tiles - 1), 0)
    done = lambda i: (jnp.maximum(i - 2, 0), 0)
    x2 = x.reshape(b * s, d)
    out = pl.pallas_call(
        functools.partial(_ret_kernel, tile=tile, tiles_per_row=s // tile, n_tiles=n_tiles),
        grid=(n_tiles + 2,),
        in_specs=[
            pl.BlockSpec((tile, d), cur),
            pl.BlockSpec((2 * tile // LANES, LANES), lambda i: (jnp.minimum(i, n_tiles - 1) // 2, 0)),
            pl.BlockSpec((tile, d), done),
            pl.BlockSpec(gains.shape, const2),
            pl.BlockSpec(inv_freq.shape, const2),
            once2(w_in.shape),
            once3(intra.shape),
            once3(cross.shape),
            once3(indec.shape),
            once3(cdec.shape),
            pl.BlockSpec(gn_gain.shape, const2),
            once2(w_out.shape),
        ],
        out_specs=pl.BlockSpec((tile, d), done),
        out_shape=jax.ShapeDtypeStruct(x2.shape, x.dtype),
        scratch_shapes=[
            pltpu.VMEM((RET_HEADS, RET_QK_DIM, RET_V_DIM), _F32),
            pltpu.VMEM((2, tile, 3 * RET_QK_TOTAL + D_MODEL), _BF16),
            pltpu.VMEM((2, tile, RET_V_TOTAL), _BF16),
        ],
        compiler_params=pltpu.CompilerParams(
            dimension_semantics=("arbitrary",), vmem_limit_bytes=VMEM_LIMIT_BYTES),
        name="retention",
    )(x2, positions.reshape(b * s // LANES, LANES), x2, gains, inv_freq, w_in, intra, cross, indec, cdec,
      gn_gain, w_out)
    return out.reshape(b, s, d)


def kernel(x, positions, norm_gain, pool_w, pool_scale, ret_w_in, ret_gn_gain, ret_w_out,
           mlp_w_up, mlp_conv_w, mlp_conv_b, mlp_w_down):
    bf = lambda a: a.astype(_BF16)
    w_up, w_down = bf(mlp_w_up), bf(mlp_w_down)
    x = _mlp_layer(x, norm_gain[0], 0, w_up, mlp_conv_w[0], mlp_conv_b[0][None, :], w_down,
                   pool_params=(bf(pool_w[0]), pool_scale[0][None, :]))
    x = _retention_layer(x, positions, norm_gain[1], bf(ret_w_in[0]), ret_gn_gain[0][None, :],
                         bf(ret_w_out[0]))
    x = _mlp_layer(x, norm_gain[1], 1, w_up, mlp_conv_w[1], mlp_conv_b[1][None, :], w_down)
    return x
```
